```python
import jax, jax.numpy as jnp
from jax import lax
import numpy as np

D_MODEL = 1024
BATCH = 8
SEQ = 2048
DEPTH = 1
DEC_BATCH = 128
DEC_SEQ = 8
PAST_LEN = 16384
PAGE_SIZE = 128

D_CONV = D_MODEL
CONV_WIDTH = 31
D_MLSTM = D_MODEL
N_HEADS = 4
HEAD_DIM = D_MLSTM // N_HEADS
CHUNK = 128
D_FF = 2816
N_MOD = 9
N_IN = 2 * D_CONV + 4 * D_MLSTM + 2 * D_MODEL + 2 * N_HEADS
EPS = 1e-6

kernel_name = 'hybrid_conv_mlstm_macaron_step'


def _rmsnorm(x, g):
    xf = x.astype(jnp.float32)
    y = xf * lax.rsqrt(jnp.mean(xf * xf, axis=-1, keepdims=True) + EPS)
    return (y * g.astype(jnp.float32)).astype(x.dtype)


def _layernorm(x, g, b):
    xf = x.astype(jnp.float32)
    mu = jnp.mean(xf, axis=-1, keepdims=True)
    var = jnp.mean(jnp.square(xf - mu), axis=-1, keepdims=True)
    y = (xf - mu) * lax.rsqrt(var + EPS)
    return (y * g.astype(jnp.float32) + b.astype(jnp.float32)).astype(x.dtype)


def _headnorm(h, g):
    y = h * lax.rsqrt(jnp.mean(h * h, axis=-1, keepdims=True) + EPS)
    return y * g.astype(jnp.float32)


def _swiglu(x, w_gate, w_up, w_down):
    return (jax.nn.silu(x @ w_gate) * (x @ w_up)) @ w_down


def _split_cols(p):
    sizes = (D_CONV, D_CONV, D_MLSTM, D_MLSTM, D_MLSTM, D_MLSTM, D_MODEL, D_MODEL, N_HEADS, N_HEADS)
    out, start = [], 0
    for s in sizes:
        out.append(p[..., start:start + s])
        start += s
    return out


def _causal_dwconv(u, buf, w, b):
    full = jnp.concatenate([buf.astype(u.dtype), u], axis=1)
    y = lax.conv_general_dilated(full, w[:, None, :].astype(u.dtype), window_strides=(1,), padding='VALID',
                                 dimension_numbers=('NWC', 'WIO', 'NWC'), feature_group_count=u.shape[-1])
    return y + b, full[:, -(CONV_WIDTH - 1):]


def _mlstm_chunkwise(q, k, v, ig, lf, C0, n0, m0, chunk):
    f32 = jnp.float32
    Bn, T, H, DK = q.shape
    nc = T // chunk
    q = q.astype(f32) * (DK ** -0.5)
    k, v, ig, lf = k.astype(f32), v.astype(f32), ig.astype(f32), lf.astype(f32)

    def to_chunks(a):
        return a.reshape((Bn, nc, chunk) + a.shape[2:]).swapaxes(0, 1)

    causal = jnp.tril(jnp.ones((chunk, chunk), dtype=bool))

    def step(carry, xs):
        C, n, m = carry
        qc, kc, vc, ic, fc = xs
        bcum = jnp.cumsum(fc, axis=1).transpose(0, 2, 1)
        ih = ic.transpose(0, 2, 1)
        dmat = bcum[:, :, :, None] - bcum[:, :, None, :] + ih[:, :, None, :]
        dmat = jnp.where(causal, dmat, -jnp.inf)
        inter = bcum + m[:, :, None]
        m_t = jnp.maximum(inter, jnp.max(dmat, axis=-1))
        w_intra = jnp.exp(dmat - m_t[..., None])
        w_inter = jnp.exp(inter - m_t)
        s = jnp.einsum('blhd,bshd->bhls', qc, kc) * w_intra
        num = jnp.einsum('bhls,bshe->blhe', s, vc) + jnp.einsum('bhl,bhed,blhd->blhe', w_inter, C, qc)
        den = jnp.sum(s, axis=-1) + w_inter * jnp.einsum('bhd,blhd->bhl', n, qc)
        den = jnp.maximum(jnp.abs(den), jnp.exp(-m_t))
        h = num / den.transpose(0, 2, 1)[..., None]
        b_last = bcum[:, :, -1]
        dec = b_last[:, :, None] - bcum + ih
        m_new = jnp.maximum(b_last + m, jnp.max(dec, axis=-1))
        a_prev = jnp.exp(b_last + m - m_new)
        ws = jnp.exp(dec - m_new[..., None])
        C_new = a_prev[..., None, None] * C + jnp.einsum('bhs,bshe,bshd->bhed', ws, vc, kc)
        n_new = a_prev[..., None] * n + jnp.einsum('bhs,bshd->bhd', ws, kc)
        return (C_new, n_new, m_new), h

    init = (C0.astype(f32), n0.astype(f32), m0.astype(f32))
    (C, n, m), hs = lax.scan(step, init, (to_chunks(q), to_chunks(k), to_chunks(v), to_chunks(ig), to_chunks(lf)))
    h = hs.swapaxes(0, 1).reshape(Bn, T, H, v.shape[-1])
    return h, C, n, m


def _layer(x, c, conv_buf, C0, n0, m0, chunk, w_ada, b_ada, norm_ffn1, w1_gate, w1_up, w1_down, norm_mix,
           w_in, b_in, w_dw, b_dw, ln_conv_g, ln_conv_b, w_conv_out, mh_norm, w_mlstm_out, w_out,
           norm_ffn2, w2_gate, w2_up, w2_down):
    Bn, T, _ = x.shape
    mod = jax.nn.silu(c) @ w_ada + b_ada
    sh1, sc1, g1, sh2, sc2, g2, sh3, sc3, g3 = [t[:, None, :] for t in jnp.split(mod, N_MOD, axis=-1)]
    h = _rmsnorm(x, norm_ffn1) * (1 + sc1) + sh1
    x = x + 0.5 * g1 * _swiglu(h, w1_gate, w1_up, w1_down)
    u = _rmsnorm(x, norm_mix) * (1 + sc2) + sh2
    p = u @ w_in + b_in
    ca, cg, q, k, v, o, ga, gb, ip, fp = _split_cols(p)
    glu = ca * jax.nn.sigmoid(cg)
    y, conv_new = _causal_dwconv(glu, conv_buf, w_dw, b_dw)
    a_out = jax.nn.silu(_layernorm(y, ln_conv_g, ln_conv_b)) @ w_conv_out
    heads = lambda t: t.reshape(Bn, T, N_HEADS, HEAD_DIM)
    hm, C, n, m = _mlstm_chunkwise(heads(q), heads(k), heads(v), ip, jax.nn.log_sigmoid(fp), C0, n0, m0, chunk)
    hm = _headnorm(hm, mh_norm).reshape(Bn, T, D_MLSTM).astype(x.dtype)
    b_out = (jax.nn.sigmoid(o) * hm) @ w_mlstm_out
    z = jax.nn.sigmoid(ga) * a_out + jax.nn.sigmoid(gb) * b_out
    x = x + g2 * (z @ w_out)
    h = _rmsnorm(x, norm_ffn2) * (1 + sc3) + sh3
    x = x + 0.5 * g3 * _swiglu(h, w2_gate, w2_up, w2_down)
    return x, conv_new, C, n, m


def setup_inputs(seed: int = 0) -> dict:
    key = jax.random.key(seed)
    keys = list(jax.random.split(key, 40))
    cnt = [0]

    def nrm(shape, scale):
        kk = keys[cnt[0]]
        cnt[0] += 1
        return scale * jax.random.normal(kk, shape, jnp.float32)

    L = DEPTH
    d_in = D_MODEL ** -0.5
    inp = {}
    inp['x_prompt'] = nrm((BATCH, SEQ, D_MODEL), 1.0)
    inp['x_sample'] = nrm((DEC_BATCH, DEC_SEQ, D_MODEL), 1.0)
    inp['c_prompt'] = nrm((BATCH, D_MODEL), 1.0)
    inp['c_sample'] = nrm((DEC_BATCH, D_MODEL), 1.0)
    inp['state_conv'] = nrm((L, DEC_BATCH, CONV_WIDTH - 1, D_CONV), 0.5)
    inp['state_C'] = nrm((L, DEC_BATCH, N_HEADS, HEAD_DIM, HEAD_DIM), 0.5)
    inp['state_n'] = nrm((L, DEC_BATCH, N_HEADS, HEAD_DIM), 0.5)
    inp['state_m'] = nrm((L, DEC_BATCH, N_HEADS), 0.5) + 1.0
    inp['w_ada'] = nrm((L, D_MODEL, N_MOD * D_MODEL), 0.5 * d_in)
    inp['b_ada'] = nrm((L, N_MOD * D_MODEL), 0.02)
    inp['norm_ffn1'] = 1.0 + nrm((L, D_MODEL), 0.02)
    inp['w1_gate'] = nrm((L, D_MODEL, D_FF), d_in)
    inp['w1_up'] = nrm((L, D_MODEL, D_FF), d_in)
    inp['w1_down'] = nrm((L, D_FF, D_MODEL), D_FF ** -0.5)
    inp['norm_mix'] = 1.0 + nrm((L, D_MODEL), 0.02)
    inp['w_in'] = nrm((L, D_MODEL, N_IN), d_in)
    inp['b_in'] = nrm((L, N_IN), 0.02).at[:, -N_HEADS:].add(jnp.linspace(3.0, 6.0, N_HEADS))
    inp['w_dw'] = nrm((L, CONV_WIDTH, D_CONV), CONV_WIDTH ** -0.5)
    inp['b_dw'] = nrm((L, D_CONV), 0.02)
    inp['ln_conv_g'] = 1.0 + nrm((L, D_CONV), 0.02)
    inp['ln_conv_b'] = nrm((L, D_CONV), 0.02)
    inp['w_conv_out'] = nrm((L, D_CONV, D_MODEL), D_CONV ** -0.5)
    inp['mh_norm'] = 1.0 + nrm((L, N_HEADS, HEAD_DIM), 0.02)
    inp['w_mlstm_out'] = nrm((L, D_MLSTM, D_MODEL), D_MLSTM ** -0.5)
    inp['w_out'] = nrm((L, D_MODEL, D_MODEL), d_in)
    inp['norm_ffn2'] = 1.0 + nrm((L, D_MODEL), 0.02)
    inp['w2_gate'] = nrm((L, D_MODEL, D_FF), d_in)
    inp['w2_up'] = nrm((L, D_MODEL, D_FF), d_in)
    inp['w2_down'] = nrm((L, D_FF, D_MODEL), D_FF ** -0.5)
    inp['final_norm'] = 1.0 + nrm((D_MODEL,), 0.02)
    return inp


def reference(x_prompt, x_sample, c_prompt, c_sample, state_conv, state_C, state_n, state_m,
              w_ada, b_ada, norm_ffn1, w1_gate, w1_up, w1_down, norm_mix, w_in, b_in, w_dw, b_dw,
              ln_conv_g, ln_conv_b, w_conv_out, mh_norm, w_mlstm_out, w_out, norm_ffn2, w2_gate, w2_up,
              w2_down, final_norm):
    f32 = jnp.float32
    bp, T = x_prompt.shape[0], x_prompt.shape[1]
    ts = x_sample.shape[1]
    yp, ys = x_prompt, x_sample
    conv_p, C_p, n_p, m_p = [], [], [], []
    conv_s, C_s, n_s, m_s = [], [], [], []
    for l in range(DEPTH):
        wl = (w_ada[l], b_ada[l], norm_ffn1[l], w1_gate[l], w1_up[l], w1_down[l], norm_mix[l], w_in[l], b_in[l],
              w_dw[l], b_dw[l], ln_conv_g[l], ln_conv_b[l], w_conv_out[l], mh_norm[l], w_mlstm_out[l], w_out[l],
              norm_ffn2[l], w2_gate[l], w2_up[l], w2_down[l])
        yp, cv, C, n, m = _layer(yp, c_prompt,
                                 jnp.zeros((bp, CONV_WIDTH - 1, D_CONV), x_prompt.dtype),
                                 jnp.zeros((bp, N_HEADS, HEAD_DIM, HEAD_DIM), f32),
                                 jnp.zeros((bp, N_HEADS, HEAD_DIM), f32),
                                 jnp.zeros((bp, N_HEADS), f32),
                                 min(CHUNK, T), *wl)
        conv_p.append(cv.astype(state_conv.dtype))
        C_p.append(C.astype(state_C.dtype))
        n_p.append(n.astype(state_n.dtype))
        m_p.append(m.astype(state_m.dtype))
        ys, cv, C, n, m = _layer(ys, c_sample, state_conv[l], state_C[l], state_n[l], state_m[l], ts, *wl)
        conv_s.append(cv.astype(state_conv.dtype))
        C_s.append(C.astype(state_C.dtype))
        n_s.append(n.astype(state_n.dtype))
        m_s.append(m.astype(state_m.dtype))
    y_prompt = _rmsnorm(yp, final_norm)
    y_sample = _rmsnorm(ys, final_norm)
    return (y_prompt, y_sample, jnp.stack(conv_p), jnp.stack(C_p), jnp.stack(n_p), jnp.stack(m_p),
            jnp.stack(conv_s), jnp.stack(C_s), jnp.stack(n_s), jnp.stack(m_s))
```

```python
import functools

import jax
import jax.numpy as jnp
from jax import lax
from jax.experimental import pallas as pl
from jax.experimental.pallas import tpu as pltpu

F32 = jnp.float32
BF16 = jnp.bfloat16

D_MODEL = 1024
N_HEADS = 4
HEAD_DIM = 256
CONV_WIDTH = 31
CONV_STATE = CONV_WIDTH - 1
D_FF = 2816
N_MOD = 9
CHUNK = 128
EPS = 1e-6

LANES = 128
SUBLANES = 8
CONV_PAD = 32
CONV_OFF = CONV_PAD - CONV_STATE
VMEM_LIMIT_BIG = 58 * 1024 * 1024
VMEM_LIMIT_SMALL = 40 * 1024 * 1024


def _sigmoid(x):
    return 1.0 / (1.0 + jnp.exp(-x))


def _log_sigmoid(x):
    return jnp.minimum(x, 0.0) - jnp.log1p(jnp.exp(-jnp.abs(x)))


def _rms(x, g):
    return x * lax.rsqrt(jnp.mean(x * x, axis=-1, keepdims=True) + EPS) * g


def _const_spec(shape):
    zeros = (0,) * len(shape)
    return pl.BlockSpec(shape, lambda *_: zeros, pipeline_mode=pl.Buffered(1))


def _mod_kernel(c_ref, w_ref, b_ref, o_ref):
    c = c_ref[...]
    a = (c * _sigmoid(c)).astype(BF16)
    o_ref[...] = jnp.dot(a, w_ref[...].astype(BF16), preferred_element_type=F32) + b_ref[...]


def _mod_call(c, w_ada, b_ada):
    rows = c.shape[0]
    n_out = w_ada.shape[1]
    bn = n_out // 8
    return pl.pallas_call(
        _mod_kernel,
        grid=(n_out // bn,),
        in_specs=[
            pl.BlockSpec((rows, D_MODEL), lambda j: (0, 0)),
            pl.BlockSpec((D_MODEL, bn), lambda j: (0, j)),
            pl.BlockSpec((1, bn), lambda j: (0, j)),
        ],
        out_specs=pl.BlockSpec((rows, bn), lambda j: (0, j)),
        out_shape=jax.ShapeDtypeStruct((rows, n_out), F32),
        compiler_params=pltpu.CompilerParams(
            dimension_semantics=("arbitrary",), vmem_limit_bytes=VMEM_LIMIT_SMALL),
        name="mod",
    )(c, w_ada, b_ada.reshape(1, n_out))


def _ffn_block(x3, nrm, sh, sc, gt, wg_ref, wu_ref, wd_ref):
    ns, t, d = x3.shape
    h = _rms(x3, nrm) * (1.0 + sc) + sh
    hb = h.reshape(ns * t, d).astype(BF16)
    g = jnp.dot(hb, wg_ref[...], preferred_element_type=F32)
    u = jnp.dot(hb, wu_ref[...], preferred_element_type=F32)
    a = (g * _sigmoid(g) * u).astype(BF16)
    dn = jnp.dot(a, wd_ref[...], preferred_element_type=F32)
    return x3 + (0.5 * gt) * dn.reshape(ns, t, d)


def _ffn_in_kernel(x_ref, mod_ref, n1_ref, wg_ref, wu_ref, wd_ref, n2_ref, wm_ref, bm_ref, wgt_ref, bg_ref,
                   x1_ref, glu_ref, q_ref, k_ref, v_ref, og_ref, ga_ref, gb_ref, gt_ref):
    x = x_ref[...]
    ns, t, d = x.shape
    mod = mod_ref[...]
    sh1, sc1, g1, sh2, sc2 = [mod[:, i:i + 1, :] for i in range(5)]
    x1 = _ffn_block(x, n1_ref[...], sh1, sc1, g1, wg_ref, wu_ref, wd_ref)
    x1_ref[...] = x1
    u = _rms(x1, n2_ref[...]) * (1.0 + sc2) + sh2
    ub = u.reshape(ns * t, d).astype(BF16)

    def proj(j):
        cols = slice(j * d, (j + 1) * d)
        return jnp.dot(ub, wm_ref[:, cols], preferred_element_type=F32) + bm_ref[:, cols]

    def put(ref, val):
        ref[...] = val.reshape(ns, t, d).astype(ref.dtype)

    put(glu_ref, proj(0) * _sigmoid(proj(1)))
    put(q_ref, proj(2) * (HEAD_DIM ** -0.5))
    put(k_ref, proj(3))
    put(v_ref, proj(4))
    put(og_ref, _sigmoid(proj(5)))
    put(ga_ref, _sigmoid(proj(6)))
    put(gb_ref, _sigmoid(proj(7)))
    gt_ref[...] = lax.dot_general(wgt_ref[...], ub, (((1,), (1,)), ((), ())),
                                  preferred_element_type=F32) + bg_ref[...]


def _ffn_in_call(x, mod_a, nrm1, wg, wu, wd, nrm2, w_main, b_main, w_gt, b_g, *, nseq, t_tile, act_dtype):
    b, s, d = x.shape
    grid = (b // nseq, s // t_tile)
    rows = nseq * t_tile
    tok = pl.BlockSpec((nseq, t_tile, d), lambda i, j: (i, j, 0))
    n_t = s // t_tile
    in_specs = [
        tok,
        pl.BlockSpec((nseq, 5, d), lambda i, j: (i, 0, 0)),
        _const_spec((1, d)),
        _const_spec(wg.shape), _const_spec(wu.shape), _const_spec(wd.shape),
        _const_spec((1, d)),
        _const_spec(w_main.shape), _const_spec(b_main.shape),
        _const_spec(w_gt.shape), _const_spec(b_g.shape),
    ]
    out_specs = [tok] * 8 + [pl.BlockSpec((2 * N_HEADS, rows), lambda i, j: (0, i * n_t + j))]
    out_shape = ([jax.ShapeDtypeStruct((b, s, d), F32)] * 2
                 + [jax.ShapeDtypeStruct((b, s, d), act_dtype)] * 6
                 + [jax.ShapeDtypeStruct((2 * N_HEADS, b * s), F32)])
    return pl.pallas_call(
        _ffn_in_kernel,
        grid=grid,
        in_specs=in_specs,
        out_specs=out_specs,
        out_shape=out_shape,
        compiler_params=pltpu.CompilerParams(
            dimension_semantics=("arbitrary", "arbitrary"), vmem_limit_bytes=VMEM_LIMIT_BIG),
        name="ffn_in",
    )(x, mod_a, nrm1, wg, wu, wd, nrm2, w_main, b_main, w_gt, b_g)


def _seg_cumsum_lanes(x, seg):
    lane = lax.broadcasted_iota(jnp.int32, x.shape, 1)
    pos = lane & (LANES - 1)
    sh = 1
    while sh < seg:
        x = x + jnp.where(pos >= sh, pltpu.roll(x, sh, 1), 0.0)
        sh *= 2
    return x


def _mlstm_head_chunk(q, k, v, g_row, lf_row, c_st, n_row, m):
    l = q.shape[0]
    t_idx = lax.broadcasted_iota(jnp.int32, (l, l), 0)
    s_idx = lax.broadcasted_iota(jnp.int32, (l, l), 1)
    causal = s_idx <= t_idx
    gb = jnp.broadcast_to(g_row, (l, l))
    lfb = jnp.broadcast_to(lf_row, (l, l))
    cm_col = jnp.max(jnp.where(causal, gb, -jnp.inf), axis=1, keepdims=True)
    bcum_col = jnp.sum(jnp.where(causal, lfb, 0.0), axis=1, keepdims=True)
    g_col = jnp.sum(jnp.where(s_idx == t_idx, gb, 0.0), axis=1, keepdims=True)
    m_col = jnp.maximum(m, cm_col)
    w_intra = jnp.where(causal, jnp.exp(gb - m_col), 0.0)
    w_inter = jnp.exp(m - m_col)
    qb, kb = q.astype(BF16), k.astype(BF16)
    s = lax.dot_general(qb, kb, (((1,), (1,)), ((), ())), preferred_element_type=F32)
    p = s * w_intra
    den_intra = jnp.sum(p, axis=1, keepdims=True)
    pv = jnp.dot(p.astype(BF16), v.astype(BF16), preferred_element_type=F32)
    qc = lax.dot_general(qb, c_st.astype(BF16), (((1,), (1,)), ((), ())), preferred_element_type=F32)
    num = pv + w_inter * qc
    qn = jnp.sum(q.astype(F32) * n_row, axis=1, keepdims=True)
    den = jnp.maximum(jnp.abs(den_intra + w_inter * qn), jnp.exp(-(bcum_col + m_col)))
    h = num / den
    m_last = m_col[l - 1:l, :]
    b_last = bcum_col[l - 1:l, :]
    a_prev = jnp.exp(m - m_last)
    ws_col = jnp.exp(g_col - m_last)
    vs = (v.astype(F32) * ws_col).astype(BF16)
    upd = lax.dot_general(vs, kb, (((0,), (0,)), ((), ())), preferred_element_type=F32)
    c_new = a_prev * c_st + upd
    n_new = a_prev * n_row + jnp.sum(ws_col * k.astype(F32), axis=0, keepdims=True)
    m_new = b_last + m_last
    return h, c_new, n_new, m_new


def _head_out(h, og, mh):
    hn = h * lax.rsqrt(jnp.mean(h * h, axis=-1, keepdims=True) + EPS) * mh
    return og.astype(F32) * hn


def _mlstm_prompt_kernel(q_ref, k_ref, v_ref, og_ref, gt_ref, mh_ref,
                         hb_ref, c_out, n_out, m_out, c_s, n_s, m_s):
    t = pl.program_id(1)
    tm = q_ref.shape[1]

    @pl.when(t == 0)
    def _():
        c_s[...] = jnp.zeros_like(c_s)
        n_s[...] = jnp.zeros_like(n_s)
        m_s[...] = jnp.zeros_like(m_s)

    g8 = gt_ref[...]
    lf8 = _log_sigmoid(g8)
    bc8 = _seg_cumsum_lanes(lf8, CHUNK)
    lf = lf8[N_HEADS:2 * N_HEADS, :]
    gg = g8[0:N_HEADS, :] - bc8[N_HEADS:2 * N_HEADS, :]
    for c in range(tm // CHUNK):
        rows = slice(c * CHUNK, (c + 1) * CHUNK)
        for hd in range(N_HEADS):
            cols = slice(hd * HEAD_DIM, (hd + 1) * HEAD_DIM)
            h, c_new, n_new, m_new = _mlstm_head_chunk(
                q_ref[0, rows, cols], k_ref[0, rows, cols], v_ref[0, rows, cols],
                gg[hd:hd + 1, rows], lf[hd:hd + 1, rows],
                c_s[hd], n_s[hd:hd + 1, :], m_s[hd:hd + 1, 0:1])
            c_s[hd] = c_new
            n_s[hd:hd + 1, :] = n_new
            m_s[hd:hd + 1, :] = jnp.broadcast_to(m_new, (1, LANES))
            hb_ref[0, rows, cols] = _head_out(h, og_ref[0, rows, cols], mh_ref[:, cols]).astype(hb_ref.dtype)

    @pl.when(t == pl.num_programs(1) - 1)
    def _():
        c_out[0] = c_s[...]
        n_out[0] = n_s[0:N_HEADS, :]
        m_out[0] = m_s[...]


def _mlstm_prompt_call(q, k, v, og, g_t, mh, *, t_tile):
    b, s, d = q.shape
    n_t = s // t_tile
    tok = pl.BlockSpec((1, t_tile, d), lambda i, j: (i, j, 0))
    return pl.pallas_call(
        _mlstm_prompt_kernel,
        grid=(b, n_t),
        in_specs=[tok, tok, tok, tok,
                  pl.BlockSpec((2 * N_HEADS, t_tile), lambda i, j: (0, i * n_t + j)),
                  _const_spec((1, d))],
        out_specs=[tok,
                   pl.BlockSpec((1, N_HEADS, HEAD_DIM, HEAD_DIM), lambda i, j: (i, 0, 0, 0)),
                   pl.BlockSpec((1, N_HEADS, HEAD_DIM), lambda i, j: (i, 0, 0)),
                   pl.BlockSpec((1, SUBLANES, LANES), lambda i, j: (i, 0, 0))],
        out_shape=[jax.ShapeDtypeStruct((b, s, d), BF16),
                   jax.ShapeDtypeStruct((b, N_HEADS, HEAD_DIM, HEAD_DIM), F32),
                   jax.ShapeDtypeStruct((b, N_HEADS, HEAD_DIM), F32),
                   jax.ShapeDtypeStruct((b, SUBLANES, LANES), F32)],
        scratch_shapes=[pltpu.VMEM((N_HEADS, HEAD_DIM, HEAD_DIM), F32),
                        pltpu.VMEM((SUBLANES, HEAD_DIM), F32),
                        pltpu.VMEM((SUBLANES, LANES), F32)],
        compiler_params=pltpu.CompilerParams(
            dimension_semantics=("arbitrary", "arbitrary"), vmem_limit_bytes=VMEM_LIMIT_SMALL),
        name="mlstm_prompt",
    )(q, k, v, og, g_t, mh)


def _mlstm_sample_kernel(q_ref, k_ref, v_ref, og_ref, gr_ref, c_in, n_in, m_in, mh_ref,
                         hb_ref, c_out, n_out, m_out):
    sb, t, _ = q_ref.shape
    lane = lax.broadcasted_iota(jnp.int32, (SUBLANES, LANES), 1)
    for s in range(sb):
        g8 = gr_ref[s]
        lf8 = _log_sigmoid(g8)
        bc8 = _seg_cumsum_lanes(lf8, t)
        lf = lf8[N_HEADS:2 * N_HEADS, :]
        gg = g8[0:N_HEADS, :] - bc8[N_HEADS:2 * N_HEADS, :]
        m_blk = m_in[s]
        m_acc = jnp.zeros((SUBLANES, LANES), F32)
        for hd in range(N_HEADS):
            cols = slice(hd * HEAD_DIM, (hd + 1) * HEAD_DIM)
            h, c_new, n_new, m_new = _mlstm_head_chunk(
                q_ref[s, :, cols], k_ref[s, :, cols], v_ref[s, :, cols],
                gg[hd:hd + 1, 0:t], lf[hd:hd + 1, 0:t],
                c_in[s, hd], n_in[s, hd:hd + 1, :], m_blk[0:1, hd:hd + 1])
            c_out[s, hd] = c_new
            n_out[s, hd:hd + 1, :] = n_new
            m_acc = jnp.where(lane == hd, m_new, m_acc)
            hb_ref[s, :, cols] = _head_out(h, og_ref[s, :, cols], mh_ref[:, cols]).astype(hb_ref.dtype)
        m_out[s] = m_acc


def _mlstm_sample_call(q, k, v, og, g_r, c0, n0, m0, mh, *, seq_block):
    b, t, d = q.shape
    tok = pl.BlockSpec((seq_block, t, d), lambda i: (i, 0, 0))
    small = pl.BlockSpec((seq_block, SUBLANES, LANES), lambda i: (i, 0, 0))
    c_spec = pl.BlockSpec((seq_block, N_HEADS, HEAD_DIM, HEAD_DIM), lambda i: (i, 0, 0, 0))
    n_spec = pl.BlockSpec((seq_block, N_HEADS, HEAD_DIM), lambda i: (i, 0, 0))
    return pl.pallas_call(
        _mlstm_sample_kernel,
        grid=(b // seq_block,),
        in_specs=[tok, tok, tok, tok, small, c_spec, n_spec, small, _const_spec((1, d))],
        out_specs=[tok, c_spec, n_spec, small],
        out_shape=[jax.ShapeDtypeStruct((b, t, d), F32),
                   jax.ShapeDtypeStruct(c0.shape, F32),
                   jax.ShapeDtypeStruct(n0.shape, F32),
                   jax.ShapeDtypeStruct((b, SUBLANES, LANES), F32)],
        compiler_params=pltpu.CompilerParams(
            dimension_semantics=("arbitrary",), vmem_limit_bytes=VMEM_LIMIT_SMALL),
        name="mlstm_sample",
    )(q, k, v, og, g_r, c0, n0, m0, mh)


def _mix_body(carry, x1_ref, glu_ref, hb_ref, ga_ref, gb_ref, mod_ref, cst_ref,
              wdw_ref, bdw_ref, lng_ref, lnb_ref, wco_ref, wmo_ref, wo_ref,
              n3_ref, wg_ref, wu_ref, wd_ref, fn_ref, y_ref, cso_ref, cbuf, ybuf):
    ns, t, d = glu_ref.shape
    rows = ns * t

    if carry:
        @pl.when(pl.program_id(1) == 0)
        def _():
            cbuf[:, 0:CONV_PAD, :] = jnp.zeros((ns, CONV_PAD, d), F32)
    else:
        cbuf[:, CONV_OFF:CONV_PAD, :] = cst_ref[...]
    cbuf[:, CONV_PAD:CONV_PAD + t, :] = glu_ref[...]

    seq_blk = SUBLANES if t == SUBLANES else 1
    row_blk = t if t == SUBLANES else 8 * SUBLANES
    for lb in range(d // LANES):
        ls = slice(lb * LANES, (lb + 1) * LANES)
        w_l = wdw_ref[:, ls]
        b_l = bdw_ref[:, ls]
        for s0 in range(0, ns, seq_blk):
            for r0 in range(0, t, row_blk):
                acc = jnp.broadcast_to(b_l, (seq_blk, row_blk, LANES))
                for j in range(CONV_WIDTH):
                    lo = CONV_OFF + r0 + j
                    acc = acc + w_l[j:j + 1, :] * cbuf[s0:s0 + seq_blk, lo:lo + row_blk, ls]
                ybuf[s0:s0 + seq_blk, r0:r0 + row_blk, ls] = acc

    cso_ref[...] = cbuf[:, CONV_OFF + t:CONV_PAD + t, :]
    if carry:
        cbuf[:, 0:CONV_PAD, :] = cbuf[:, t:t + CONV_PAD, :]

    yc = ybuf[...].reshape(rows, d)
    mu = jnp.mean(yc, axis=-1, keepdims=True)
    yd = yc - mu
    var = jnp.mean(yd * yd, axis=-1, keepdims=True)
    ln = yd * lax.rsqrt(var + EPS) * lng_ref[...] + lnb_ref[...]
    a_in = (ln * _sigmoid(ln)).astype(BF16)
    a_out = jnp.dot(a_in, wco_ref[...], preferred_element_type=F32)
    b_out = jnp.dot(hb_ref[...].reshape(rows, d).astype(BF16), wmo_ref[...], preferred_element_type=F32)
    z = (ga_ref[...].reshape(rows, d).astype(F32) * a_out
         + gb_ref[...].reshape(rows, d).astype(F32) * b_out).astype(BF16)
    zo = jnp.dot(z, wo_ref[...], preferred_element_type=F32)

    mod = mod_ref[...]
    g2, sh3, sc3, g3 = [mod[:, i:i + 1, :] for i in range(4)]
    x2 = x1_ref[...] + g2 * zo.reshape(ns, t, d)
    x3 = _ffn_block(x2, n3_ref[...], sh3, sc3, g3, wg_ref, wu_ref, wd_ref)
    y_ref[...] = _rms(x3, fn_ref[...])


def _mix_prompt_kernel(x1_ref, glu_ref, hb_ref, ga_ref, gb_ref, mod_ref, *rest):
    _mix_body(True, x1_ref, glu_ref, hb_ref, ga_ref, gb_ref, mod_ref, None, *rest)


def _mix_sample_kernel(x1_ref, glu_ref, hb_ref, ga_ref, gb_ref, mod_ref, cst_ref, *rest):
    _mix_body(False, x1_ref, glu_ref, hb_ref, ga_ref, gb_ref, mod_ref, cst_ref, *rest)


def _mix_call(x1, glu, hb, ga, gb, mod_b, conv_state, weights, *, nseq, t_tile):
    b, s, d = x1.shape
    carry = conv_state is None
    grid = (b // nseq, s // t_tile)
    tok = pl.BlockSpec((nseq, t_tile, d), lambda i, j: (i, j, 0))
    st_spec = pl.BlockSpec((nseq, CONV_STATE, d), lambda i, j: (i, 0, 0))
    in_specs = [tok] * 5 + [pl.BlockSpec((nseq, 4, d), lambda i, j: (i, 0, 0))]
    args = [x1, glu, hb, ga, gb, mod_b]
    if not carry:
        in_specs.append(st_spec)
        args.append(conv_state)
    in_specs += [_const_spec(w.shape) for w in weights]
    args += list(weights)
    return pl.pallas_call(
        _mix_prompt_kernel if carry else _mix_sample_kernel,
        grid=grid,
        in_specs=in_specs,
        out_specs=[tok, st_spec],
        out_shape=[jax.ShapeDtypeStruct((b, s, d), F32),
                   jax.ShapeDtypeStruct((b, CONV_STATE, d), F32)],
        scratch_shapes=[pltpu.VMEM((nseq, CONV_PAD + t_tile, d), F32),
                        pltpu.VMEM((nseq, t_tile, d), F32)],
        compiler_params=pltpu.CompilerParams(
            dimension_semantics=("arbitrary", "arbitrary"), vmem_limit_bytes=VMEM_LIMIT_BIG),
        name="mix_prompt" if carry else "mix_sample",
    )(*args)


def _layer(xp, xs, c_all, st_conv, st_c, st_n, st_m, final_norm,
           w_ada, b_ada, norm_ffn1, w1_gate, w1_up, w1_down, norm_mix, w_in, b_in, w_dw, b_dw,
           ln_conv_g, ln_conv_b, w_conv_out, mh_norm, w_mlstm_out, w_out, norm_ffn2, w2_gate, w2_up, w2_down):
    bp = xp.shape[0]
    bs, ts, d = xs.shape
    n_main = 8 * d
    row = lambda a: a.reshape(1, -1)

    mod = _mod_call(c_all, w_ada, b_ada).reshape(-1, N_MOD, d)
    mod_a, mod_b = mod[:, 0:5, :], mod[:, 5:N_MOD, :]

    w_main = w_in[:, :n_main].astype(BF16)
    b_main = row(b_in[:n_main])
    w_gt = w_in[:, n_main:].T.astype(BF16)
    b_g = b_in[n_main:].reshape(-1, 1)
    ffn1 = (row(norm_ffn1), w1_gate.astype(BF16), w1_up.astype(BF16), w1_down.astype(BF16))
    in_w = (row(norm_mix), w_main, b_main, w_gt, b_g)
    w_dw_pad = jnp.concatenate([w_dw, jnp.zeros((CONV_PAD - CONV_WIDTH, d), F32)], axis=0)
    mix_w = (w_dw_pad, row(b_dw), row(ln_conv_g), row(ln_conv_b),
             w_conv_out.astype(BF16), w_mlstm_out.astype(BF16), w_out.astype(BF16),
             row(norm_ffn2), w2_gate.astype(BF16), w2_up.astype(BF16), w2_down.astype(BF16), row(final_norm))
    mh = row(mh_norm)

    x1, glu, q, k, v, og, ga, gb, g_t = _ffn_in_call(
        xp, mod_a[:bp], *ffn1, *in_w, nseq=1, t_tile=256, act_dtype=BF16)
    hb, c_p, n_p, m_p = _mlstm_prompt_call(q, k, v, og, g_t, mh, t_tile=256)
    y_p, conv_p = _mix_call(x1, glu, hb, ga, gb, mod_b[:bp], None, mix_w, nseq=1, t_tile=256)

    x1, glu, q, k, v, og, ga, gb, g_t = _ffn_in_call(
        xs, mod_a[bp:], *ffn1, *in_w, nseq=32, t_tile=ts, act_dtype=F32)
    g_r = g_t.reshape(2 * N_HEADS, bs, ts).transpose(1, 0, 2)
    g_r = jnp.pad(g_r, ((0, 0), (0, 0), (0, LANES - ts)))
    m_b = jnp.broadcast_to(jnp.pad(st_m, ((0, 0), (0, LANES - N_HEADS)))[:, None, :], (bs, SUBLANES, LANES))
    hb, c_s, n_s, m_s = _mlstm_sample_call(q, k, v, og, g_r, st_c, st_n, m_b, mh, seq_block=4)
    y_s, conv_s = _mix_call(x1, glu, hb, ga, gb, mod_b[bp:], st_conv, mix_w, nseq=16, t_tile=ts)

    return (y_p, y_s, conv_p, c_p, n_p, m_p[:, :N_HEADS, 0], conv_s, c_s, n_s, m_s[:, 0, :N_HEADS])


def kernel(x_prompt, x_sample, c_prompt, c_sample, state_conv, state_C, state_n, state_m, w_ada, b_ada, norm_ffn1, w1_gate, w1_up, w1_down, norm_mix, w_in, b_in, w_dw, b_dw, ln_conv_g, ln_conv_b, w_conv_out, mh_norm, w_mlstm_out, w_out, norm_ffn2, w2_gate, w2_up, w2_down, final_norm):
    depth = w_ada.shape[0]
    assert depth == 1, "the fused final norm assumes a single layer"
    c_all = jnp.concatenate([c_prompt, c_sample], axis=0)
    layer_w = (w_ada, b_ada, norm_ffn1, w1_gate, w1_up, w1_down, norm_mix, w_in, b_in, w_dw, b_dw,
               ln_conv_g, ln_conv_b, w_conv_out, mh_norm, w_mlstm_out, w_out, norm_ffn2, w2_gate, w2_up, w2_down)
    outs = _layer(x_prompt, x_sample, c_all, state_conv[0], state_C[0], state_n[0], state_m[0], final_norm,
                  *[w[0] for w in layer_w])
    y_p, y_s = outs[0], outs[1]
    return (y_p, y_s) + tuple(o[None] for o in outs[2:])
```

```python
import functools

import jax
import jax.numpy as jnp
from jax import lax
from jax.experimental import pallas as pl
from jax.experimental.pallas import tpu as pltpu

F32 = jnp.float32
BF16 = jnp.bfloat16

D_MODEL = 1024
N_HEADS = 4
HEAD_DIM = 256
CONV_WIDTH = 31
CONV_STATE = CONV_WIDTH - 1
D_FF = 2816
N_MOD = 9
CHUNK = 128
EPS = 1e-6

LANES = 128
SUBLANES = 8
CONV_PAD = 32
CONV_OFF = CONV_PAD - CONV_STATE
VMEM_LIMIT_BIG = 58 * 1024 * 1024
VMEM_LIMIT_SMALL = 40 * 1024 * 1024


def _sigmoid(x):
    return 1.0 / (1.0 + jnp.exp(-x))


def _log_sigmoid(x):
    return jnp.minimum(x, 0.0) - jnp.log1p(jnp.exp(-jnp.abs(x)))


def _rms(x, g):
    return x * lax.rsqrt(jnp.mean(x * x, axis=-1, keepdims=True) + EPS) * g


def _const_spec(shape):
    zeros = (0,) * len(shape)
    return pl.BlockSpec(shape, lambda *_: zeros, pipeline_mode=pl.Buffered(1))


def _mod_kernel(c_ref, w_ref, b_ref, o_ref):
    c = c_ref[...]
    a = (c * _sigmoid(c)).astype(BF16)
    o_ref[...] = jnp.dot(a, w_ref[...].astype(BF16), preferred_element_type=F32) + b_ref[...]


def _mod_call(c, w_ada, b_ada):
    rows = c.shape[0]
    n_out = w_ada.shape[1]
    bn = n_out // 8
    return pl.pallas_call(
        _mod_kernel,
        grid=(n_out // bn,),
        in_specs=[
            pl.BlockSpec((rows, D_MODEL), lambda j: (0, 0)),
            pl.BlockSpec((D_MODEL, bn), lambda j: (0, j)),
            pl.BlockSpec((1, bn), lambda j: (0, j)),
        ],
        out_specs=pl.BlockSpec((rows, bn), lambda j: (0, j)),
        out_shape=jax.ShapeDtypeStruct((rows, n_out), F32),
        compiler_params=pltpu.CompilerParams(
            dimension_semantics=("arbitrary",), vmem_limit_bytes=VMEM_LIMIT_SMALL),
        name="mod",
    )(c, w_ada, b_ada.reshape(1, n_out))


def _ffn_block(x3, nrm, sh, sc, gt, wg_ref, wu_ref, wd_ref):
    ns, t, d = x3.shape
    h = _rms(x3, nrm) * (1.0 + sc) + sh
    hb = h.reshape(ns * t, d).astype(BF16)
    g = jnp.dot(hb, wg_ref[...], preferred_element_type=F32)
    u = jnp.dot(hb, wu_ref[...], preferred_element_type=F32)
    a = (g * _sigmoid(g) * u).astype(BF16)
    dn = jnp.dot(a, wd_ref[...], preferred_element_type=F32)
    marks = [m[0:SUBLANES, c:c + LANES] for m in (g, u) for c in range(0, m.shape[1], LANES)]
    return x3 + (0.5 * gt) * dn.reshape(ns, t, d), marks


def _zero_after(x):
    bits = lax.bitcast_convert_type(x, jnp.uint32)
    bits = lax.shift_right_logical(lax.shift_right_logical(bits, jnp.uint32(16)), jnp.uint32(16))
    return lax.bitcast_convert_type(bits, F32)


def _ffn_in_kernel(x_ref, mod_ref, n1_ref, wg_ref, wu_ref, wd_ref, n2_ref, wm_ref, bm_ref, wgt_ref, bg_ref,
                   x1_ref, glu_ref, q_ref, k_ref, v_ref, og_ref, ga_ref, gb_ref, gt_ref):
    x = x_ref[...]
    ns, t, d = x.shape
    mod = mod_ref[...]
    sh1, sc1, g1, sh2, sc2 = [mod[:, i:i + 1, :] for i in range(5)]
    x1, _ = _ffn_block(x, n1_ref[...], sh1, sc1, g1, wg_ref, wu_ref, wd_ref)
    x1_ref[...] = x1
    u = _rms(x1, n2_ref[...]) * (1.0 + sc2) + sh2
    ub = u.reshape(ns * t, d).astype(BF16)

    def proj(j):
        cols = slice(j * d, (j + 1) * d)
        return jnp.dot(ub, wm_ref[:, cols], preferred_element_type=F32) + bm_ref[:, cols]

    def put(ref, val):
        ref[...] = val.reshape(ns, t, d).astype(ref.dtype)

    put(glu_ref, proj(0) * _sigmoid(proj(1)))
    put(q_ref, proj(2) * (HEAD_DIM ** -0.5))
    put(k_ref, proj(3))
    put(v_ref, proj(4))
    put(og_ref, _sigmoid(proj(5)))
    put(ga_ref, _sigmoid(proj(6)))
    put(gb_ref, _sigmoid(proj(7)))
    gt_ref[...] = lax.dot_general(wgt_ref[...], ub, (((1,), (1,)), ((), ())),
                                  preferred_element_type=F32) + bg_ref[...]


def _ffn_in_call(x, mod_a, nrm1, wg, wu, wd, nrm2, w_main, b_main, w_gt, b_g, *, nseq, t_tile, act_dtype):
    b, s, d = x.shape
    grid = (b // nseq, s // t_tile)
    rows = nseq * t_tile
    tok = pl.BlockSpec((nseq, t_tile, d), lambda i, j: (i, j, 0))
    n_t = s // t_tile
    in_specs = [
        tok,
        pl.BlockSpec((nseq, 5, d), lambda i, j: (i, 0, 0)),
        _const_spec((1, d)),
        _const_spec(wg.shape), _const_spec(wu.shape), _const_spec(wd.shape),
        _const_spec((1, d)),
        _const_spec(w_main.shape), _const_spec(b_main.shape),
        _const_spec(w_gt.shape), _const_spec(b_g.shape),
    ]
    out_specs = [tok] * 8 + [pl.BlockSpec((2 * N_HEADS, rows), lambda i, j: (0, i * n_t + j))]
    out_shape = ([jax.ShapeDtypeStruct((b, s, d), F32)] * 2
                 + [jax.ShapeDtypeStruct((b, s, d), act_dtype)] * 6
                 + [jax.ShapeDtypeStruct((2 * N_HEADS, b * s), F32)])
    return pl.pallas_call(
        _ffn_in_kernel,
        grid=grid,
        in_specs=in_specs,
        out_specs=out_specs,
        out_shape=out_shape,
        compiler_params=pltpu.CompilerParams(
            dimension_semantics=("arbitrary", "arbitrary"), vmem_limit_bytes=VMEM_LIMIT_BIG),
        name="ffn_in",
    )(x, mod_a, nrm1, wg, wu, wd, nrm2, w_main, b_main, w_gt, b_g)


def _seg_cumsum_lanes(x, seg):
    lane = lax.broadcasted_iota(jnp.int32, x.shape, 1)
    pos = lane & (LANES - 1)
    sh = 1
    while sh < seg:
        x = x + jnp.where(pos >= sh, pltpu.roll(x, sh, 1), 0.0)
        sh *= 2
    return x


def _mlstm_head_chunk(q, k, v, g_row, lf_row, c_st, n_row, m):
    l = q.shape[0]
    t_idx = lax.broadcasted_iota(jnp.int32, (l, l), 0)
    s_idx = lax.broadcasted_iota(jnp.int32, (l, l), 1)
    causal = s_idx <= t_idx
    gb = jnp.broadcast_to(g_row, (l, l))
    lfb = jnp.broadcast_to(lf_row, (l, l))
    cm_col = jnp.max(jnp.where(causal, gb, -jnp.inf), axis=1, keepdims=True)
    bcum_col = jnp.sum(jnp.where(causal, lfb, 0.0), axis=1, keepdims=True)
    g_col = jnp.sum(jnp.where(s_idx == t_idx, gb, 0.0), axis=1, keepdims=True)
    g_max = cm_col[l - 1:l, :]
    qb, kb, vb = q.astype(BF16), k.astype(BF16), v.astype(BF16)

    s = lax.dot_general(qb, kb, (((1,), (1,)), ((), ())), preferred_element_type=F32)
    p = (s * jnp.where(causal, jnp.exp(gb - cm_col), 0.0)).astype(BF16)
    pv = jnp.dot(p, vb, preferred_element_type=F32)
    den_loc = jnp.dot(p, jnp.ones((l, LANES), BF16), preferred_element_type=F32)
    ws_loc = jnp.exp(g_col - g_max)
    vs = (v.astype(F32) * ws_loc).astype(BF16)
    upd = lax.dot_general(vs, kb, (((0,), (0,)), ((), ())), preferred_element_type=F32)
    n_upd = jnp.sum(ws_loc * k.astype(F32), axis=0, keepdims=True)

    m_col = jnp.maximum(m, cm_col)
    r_col = jnp.exp(cm_col - m_col)
    w_inter = jnp.exp(m - m_col)
    qc = lax.dot_general(qb, c_st.astype(BF16), (((1,), (1,)), ((), ())), preferred_element_type=F32)
    n_rep = jnp.broadcast_to(n_row, (LANES, n_row.shape[1])).astype(BF16)
    qn = lax.dot_general(qb, n_rep, (((1,), (1,)), ((), ())), preferred_element_type=F32)
    num = r_col * pv + w_inter * qc
    den = jnp.maximum(jnp.abs(r_col * den_loc + w_inter * qn), jnp.exp(-(bcum_col + m_col)))
    inv_den = 1.0 / den
    h = num * jnp.concatenate([inv_den] * (num.shape[1] // LANES), axis=1)
    m_last = m_col[l - 1:l, :]
    a_prev = jnp.exp(m - m_last)
    scale = jnp.exp(g_max - m_last)
    c_new = a_prev * c_st + scale * upd
    n_new = a_prev * n_row + scale * n_upd
    m_new = bcum_col[l - 1:l, :] + m_last
    return h, c_new, n_new, m_new


def _head_out(h, og, mh):
    hn = h * lax.rsqrt(jnp.mean(h * h, axis=-1, keepdims=True) + EPS) * mh
    return og.astype(F32) * hn


def _mlstm_prompt_kernel(q_ref, k_ref, v_ref, og_ref, gt_ref, mh_ref,
                         hb_ref, c_out, n_out, m_out, c_s, n_s, m_s):
    t = pl.program_id(1)
    tm = q_ref.shape[1]

    @pl.when(t == 0)
    def _():
        c_s[...] = jnp.zeros_like(c_s)
        n_s[...] = jnp.zeros_like(n_s)
        m_s[...] = jnp.zeros_like(m_s)

    g8 = gt_ref[...]
    lf8 = _log_sigmoid(g8)
    bc8 = _seg_cumsum_lanes(lf8, CHUNK)
    lf = lf8[N_HEADS:2 * N_HEADS, :]
    gg = g8[0:N_HEADS, :] - bc8[N_HEADS:2 * N_HEADS, :]
    for c in range(tm // CHUNK):
        rows = slice(c * CHUNK, (c + 1) * CHUNK)
        for hd in range(N_HEADS):
            cols = slice(hd * HEAD_DIM, (hd + 1) * HEAD_DIM)
            h, c_new, n_new, m_new = _mlstm_head_chunk(
                q_ref[0, rows, cols], k_ref[0, rows, cols], v_ref[0, rows, cols],
                gg[hd:hd + 1, rows], lf[hd:hd + 1, rows],
                c_s[hd], n_s[hd:hd + 1, :], m_s[hd:hd + 1, 0:1])
            c_s[hd] = c_new
            n_s[hd:hd + 1, :] = n_new
            m_s[hd:hd + 1, :] = jnp.broadcast_to(m_new, (1, LANES))
            hb_ref[0, rows, cols] = _head_out(h, og_ref[0, rows, cols], mh_ref[:, cols]).astype(hb_ref.dtype)

    @pl.when(t == pl.num_programs(1) - 1)
    def _():
        c_out[0] = c_s[...]
        n_out[0] = n_s[0:N_HEADS, :]
        m_out[0] = m_s[...]


def _mlstm_prompt_call(q, k, v, og, g_t, mh, *, t_tile):
    b, s, d = q.shape
    n_t = s // t_tile
    tok = pl.BlockSpec((1, t_tile, d), lambda i, j: (i, j, 0))
    return pl.pallas_call(
        _mlstm_prompt_kernel,
        grid=(b, n_t),
        in_specs=[tok, tok, tok, tok,
                  pl.BlockSpec((2 * N_HEADS, t_tile), lambda i, j: (0, i * n_t + j)),
                  _const_spec((1, d))],
        out_specs=[tok,
                   pl.BlockSpec((1, N_HEADS, HEAD_DIM, HEAD_DIM), lambda i, j: (i, 0, 0, 0)),
                   pl.BlockSpec((1, N_HEADS, HEAD_DIM), lambda i, j: (i, 0, 0)),
                   pl.BlockSpec((1, SUBLANES, LANES), lambda i, j: (i, 0, 0))],
        out_shape=[jax.ShapeDtypeStruct((b, s, d), BF16),
                   jax.ShapeDtypeStruct((b, N_HEADS, HEAD_DIM, HEAD_DIM), F32),
                   jax.ShapeDtypeStruct((b, N_HEADS, HEAD_DIM), F32),
                   jax.ShapeDtypeStruct((b, SUBLANES, LANES), F32)],
        scratch_shapes=[pltpu.VMEM((N_HEADS, HEAD_DIM, HEAD_DIM), F32),
                        pltpu.VMEM((SUBLANES, HEAD_DIM), F32),
                        pltpu.VMEM((SUBLANES, LANES), F32)],
        compiler_params=pltpu.CompilerParams(
            dimension_semantics=("arbitrary", "arbitrary"), vmem_limit_bytes=VMEM_LIMIT_SMALL),
        name="mlstm_prompt",
    )(q, k, v, og, g_t, mh)


def _mlstm_sample_kernel(q_ref, k_ref, v_ref, og_ref, gr_ref, c_in, n_in, m_in, mh_ref,
                         hb_ref, c_out, n_out, m_out):
    sb, t, _ = q_ref.shape
    lane = lax.broadcasted_iota(jnp.int32, (SUBLANES, LANES), 1)
    for s in range(sb):
        g8 = gr_ref[s]
        lf8 = _log_sigmoid(g8)
        bc8 = _seg_cumsum_lanes(lf8, t)
        lf = lf8[N_HEADS:2 * N_HEADS, :]
        gg = g8[0:N_HEADS, :] - bc8[N_HEADS:2 * N_HEADS, :]
        m_blk = m_in[s]
        m_acc = jnp.zeros((SUBLANES, LANES), F32)
        for hd in range(N_HEADS):
            cols = slice(hd * HEAD_DIM, (hd + 1) * HEAD_DIM)
            h, c_new, n_new, m_new = _mlstm_head_chunk(
                q_ref[s, :, cols], k_ref[s, :, cols], v_ref[s, :, cols],
                gg[hd:hd + 1, 0:t], lf[hd:hd + 1, 0:t],
                c_in[s, hd], n_in[s, hd:hd + 1, :], m_blk[0:1, hd:hd + 1])
            c_out[s, hd] = c_new
            n_out[s, hd:hd + 1, :] = n_new
            m_acc = jnp.where(lane == hd, m_new, m_acc)
            hb_ref[s, :, cols] = _head_out(h, og_ref[s, :, cols], mh_ref[:, cols]).astype(hb_ref.dtype)
        m_out[s] = m_acc


def _mlstm_sample_call(q, k, v, og, g_r, c0, n0, m0, mh, *, seq_block):
    b, t, d = q.shape
    tok = pl.BlockSpec((seq_block, t, d), lambda i: (i, 0, 0))
    small = pl.BlockSpec((seq_block, SUBLANES, LANES), lambda i: (i, 0, 0))
    c_spec = pl.BlockSpec((seq_block, N_HEADS, HEAD_DIM, HEAD_DIM), lambda i: (i, 0, 0, 0))
    n_spec = pl.BlockSpec((seq_block, N_HEADS, HEAD_DIM), lambda i: (i, 0, 0))
    return pl.pallas_call(
        _mlstm_sample_kernel,
        grid=(b // seq_block,),
        in_specs=[tok, tok, tok, tok, small, c_spec, n_spec, small, _const_spec((1, d))],
        out_specs=[tok, c_spec, n_spec, small],
        out_shape=[jax.ShapeDtypeStruct((b, t, d), F32),
                   jax.ShapeDtypeStruct(c0.shape, F32),
                   jax.ShapeDtypeStruct(n0.shape, F32),
                   jax.ShapeDtypeStruct((b, SUBLANES, LANES), F32)],
        compiler_params=pltpu.CompilerParams(
            dimension_semantics=("arbitrary",), vmem_limit_bytes=VMEM_LIMIT_SMALL),
        name="mlstm_sample",
    )(q, k, v, og, g_r, c0, n0, m0, mh)


def _mix_tail(x1, yc3, hb, ga, gb, mod, lng_ref, lnb_ref, wco_ref, wmo_ref, wo_ref,
              n3_ref, wg_ref, wu_ref, wd_ref, fn_ref):
    ns, t, d = x1.shape
    rows = ns * t
    yc = yc3.reshape(rows, d)
    mu = jnp.mean(yc, axis=-1, keepdims=True)
    yd = yc - mu
    var = jnp.mean(yd * yd, axis=-1, keepdims=True)
    ln = yd * lax.rsqrt(var + EPS) * lng_ref[...] + lnb_ref[...]
    a_in = (ln * _sigmoid(ln)).astype(BF16)
    a_out = jnp.dot(a_in, wco_ref[...], preferred_element_type=F32)
    b_out = jnp.dot(hb.reshape(rows, d).astype(BF16), wmo_ref[...], preferred_element_type=F32)
    z = (ga.reshape(rows, d).astype(F32) * a_out + gb.reshape(rows, d).astype(F32) * b_out).astype(BF16)
    zo = jnp.dot(z, wo_ref[...], preferred_element_type=F32)
    g2, sh3, sc3, g3 = [mod[:, i:i + 1, :] for i in range(4)]
    x2 = x1 + g2 * zo.reshape(ns, t, d)
    x3, ffn_marks = _ffn_block(x2, n3_ref[...], sh3, sc3, g3, wg_ref, wu_ref, wd_ref)
    return _rms(x3, fn_ref[...]), ffn_marks


def _mix_sample_kernel(x1_ref, glu_ref, hb_ref, ga_ref, gb_ref, mod_ref, cst_ref, wdw_ref, bdw_ref,
                       *rest):
    *tail_w, y_ref, cso_ref, cbuf, ybuf = rest
    ns, t, d = glu_ref.shape
    cbuf[:, CONV_OFF:CONV_PAD, :] = cst_ref[...]
    cbuf[:, CONV_PAD:CONV_PAD + t, :] = glu_ref[...]
    for lb in range(d // LANES):
        ls = slice(lb * LANES, (lb + 1) * LANES)
        w_l = wdw_ref[:, ls]
        b_l = bdw_ref[:, ls]
        for s0 in range(0, ns, SUBLANES):
            acc = jnp.broadcast_to(b_l, (SUBLANES, t, LANES))
            for j in range(CONV_WIDTH):
                lo = CONV_OFF + j
                acc = acc + w_l[j:j + 1, :] * cbuf[s0:s0 + SUBLANES, lo:lo + t, ls]
            ybuf[s0:s0 + SUBLANES, :, ls] = acc
    cso_ref[...] = cbuf[:, CONV_OFF + t:CONV_PAD + t, :]
    y_ref[...], _ = _mix_tail(x1_ref[...], ybuf[...], hb_ref[...], ga_ref[...], gb_ref[...], mod_ref[...],
                              *tail_w)


CONV_ROW_BLOCK = 8 * SUBLANES


def _mix_prompt_kernel(n_t, glu_ref, x1_ref, hb_ref, ga_ref, gb_ref, mod_ref, wdw_ref, bdw_ref, *rest):
    *tail_w, y_ref, cso_ref, cbuf, xs, ybuf = rest
    _, t, d = glu_ref.shape
    s = pl.program_id(0)
    n_rows = CONV_PAD + t
    span = n_rows - SUBLANES

    @pl.when(s == 0)
    def _():
        ybuf[...] = jnp.zeros_like(ybuf)
        cbuf[...] = jnp.zeros_like(cbuf)

    y_ref[...], ffn_marks = _mix_tail(x1_ref[...], ybuf[...][None], hb_ref[...], ga_ref[...], gb_ref[...],
                                      mod_ref[...], *tail_w)
    n_lb = d // LANES
    n_rb = t // CONV_ROW_BLOCK
    holds = [_zero_after(ffn_marks[(p * len(ffn_marks)) // (n_lb * n_rb)]) for p in range(n_lb * n_rb)]

    first = (s % n_t) == 0
    cbuf[0:CONV_PAD, :] = jnp.where(first, 0.0, cbuf[t:t + CONV_PAD, :])
    cbuf[CONV_PAD:n_rows, :] = glu_ref[0]
    for lb in range(n_lb):
        ls = slice(lb * LANES, (lb + 1) * LANES)
        xl = xs.at[lb % 2]
        hist = (cbuf[:, ls].reshape(n_rows // SUBLANES, SUBLANES, LANES) + holds[lb * n_rb]).reshape(n_rows, LANES)
        xl[0] = hist
        for r in range(1, SUBLANES):
            xl[r, 0:span, :] = hist[r:r + span, :]
        w_raw = wdw_ref[:, ls].reshape(CONV_PAD // SUBLANES, SUBLANES, LANES)
        b_l = bdw_ref[:, ls]
        for rb, r0 in enumerate(range(0, t, CONV_ROW_BLOCK)):
            w_l = (w_raw + holds[lb * n_rb + rb]).reshape(CONV_PAD, LANES)
            acc = jnp.broadcast_to(b_l, (CONV_ROW_BLOCK, LANES))
            for j in range(CONV_WIDTH):
                r = (CONV_OFF + j) % SUBLANES
                lo = CONV_OFF + j - r + r0
                acc = acc + w_l[j:j + 1, :] * xl[r, lo:lo + CONV_ROW_BLOCK, :]
            ybuf[r0:r0 + CONV_ROW_BLOCK, ls] = acc

    @pl.when((s % n_t) == n_t - 1)
    def _():
        cso_ref[0] = cbuf[CONV_OFF + t:n_rows, :]


def _mix_prompt_call(x1, glu, hb, ga, gb, mod_b, weights, *, t_tile):
    b, s, d = x1.shape
    n_t = s // t_tile
    n_tiles = b * n_t

    def conv_tile(i):
        c = jnp.minimum(i, n_tiles - 1)
        return c // n_t, c % n_t, 0

    def tail_tile(i):
        p = jnp.maximum(i - 1, 0)
        return p // n_t, p % n_t, 0

    tail_tok = pl.BlockSpec((1, t_tile, d), tail_tile)
    in_specs = ([pl.BlockSpec((1, t_tile, d), conv_tile)] + [tail_tok] * 4
                + [pl.BlockSpec((1, 4, d), lambda i: (jnp.maximum(i - 1, 0) // n_t, 0, 0))]
                + [_const_spec(w.shape) for w in weights])
    return pl.pallas_call(
        functools.partial(_mix_prompt_kernel, n_t),
        grid=(n_tiles + 1,),
        in_specs=in_specs,
        out_specs=[tail_tok,
                   pl.BlockSpec((1, CONV_STATE, d), lambda i: (jnp.minimum(i, n_tiles - 1) // n_t, 0, 0))],
        out_shape=[jax.ShapeDtypeStruct((b, s, d), F32),
                   jax.ShapeDtypeStruct((b, CONV_STATE, d), F32)],
        scratch_shapes=[pltpu.VMEM((CONV_PAD + t_tile, d), F32),
                        pltpu.VMEM((2, SUBLANES, CONV_PAD + t_tile, LANES), F32),
                        pltpu.VMEM((t_tile, d), F32)],
        compiler_params=pltpu.CompilerParams(
            dimension_semantics=("arbitrary",), vmem_limit_bytes=VMEM_LIMIT_BIG),
        name="mix_prompt",
    )(glu, x1, hb, ga, gb, mod_b, *weights)


def _mix_sample_call(x1, glu, hb, ga, gb, mod_b, conv_state, weights, *, nseq):
    b, t, d = x1.shape
    tok = pl.BlockSpec((nseq, t, d), lambda i: (i, 0, 0))
    st_spec = pl.BlockSpec((nseq, CONV_STATE, d), lambda i: (i, 0, 0))
    in_specs = ([tok] * 5 + [pl.BlockSpec((nseq, 4, d), lambda i: (i, 0, 0)), st_spec]
                + [_const_spec(w.shape) for w in weights])
    return pl.pallas_call(
        _mix_sample_kernel,
        grid=(b // nseq,),
        in_specs=in_specs,
        out_specs=[tok, st_spec],
        out_shape=[jax.ShapeDtypeStruct((b, t, d), F32),
                   jax.ShapeDtypeStruct((b, CONV_STATE, d), F32)],
        scratch_shapes=[pltpu.VMEM((nseq, CONV_PAD + t, d), F32),
                        pltpu.VMEM((nseq, t, d), F32)],
        compiler_params=pltpu.CompilerParams(
            dimension_semantics=("arbitrary",), vmem_limit_bytes=VMEM_LIMIT_BIG),
        name="mix_sample",
    )(x1, glu, hb, ga, gb, mod_b, conv_state, *weights)


def _layer(xp, xs, c_all, st_conv, st_c, st_n, st_m, final_norm,
           w_ada, b_ada, norm_ffn1, w1_gate, w1_up, w1_down, norm_mix, w_in, b_in, w_dw, b_dw,
           ln_conv_g, ln_conv_b, w_conv_out, mh_norm, w_mlstm_out, w_out, norm_ffn2, w2_gate, w2_up, w2_down):
    bp = xp.shape[0]
    bs, ts, d = xs.shape
    n_main = 8 * d
    row = lambda a: a.reshape(1, -1)

    mod = _mod_call(c_all, w_ada, b_ada).reshape(-1, N_MOD, d)
    mod_a, mod_b = mod[:, 0:5, :], mod[:, 5:N_MOD, :]

    w_main = w_in[:, :n_main].astype(BF16)
    b_main = row(b_in[:n_main])
    w_gt = w_in[:, n_main:].T.astype(BF16)
    b_g = b_in[n_main:].reshape(-1, 1)
    ffn1 = (row(norm_ffn1), w1_gate.astype(BF16), w1_up.astype(BF16), w1_down.astype(BF16))
    in_w = (row(norm_mix), w_main, b_main, w_gt, b_g)
    w_dw_pad = jnp.concatenate([w_dw, jnp.zeros((CONV_PAD - CONV_WIDTH, d), F32)], axis=0)
    mix_w = (w_dw_pad, row(b_dw), row(ln_conv_g), row(ln_conv_b),
             w_conv_out.astype(BF16), w_mlstm_out.astype(BF16), w_out.astype(BF16),
             row(norm_ffn2), w2_gate.astype(BF16), w2_up.astype(BF16), w2_down.astype(BF16), row(final_norm))
    mh = row(mh_norm)

    x1, glu, q, k, v, og, ga, gb, g_t = _ffn_in_call(
        xp, mod_a[:bp], *ffn1, *in_w, nseq=1, t_tile=256, act_dtype=BF16)
    hb, c_p, n_p, m_p = _mlstm_prompt_call(q, k, v, og, g_t, mh, t_tile=512)
    y_p, conv_p = _mix_prompt_call(x1, glu, hb, ga, gb, mod_b[:bp], mix_w, t_tile=256)

    x1, glu, q, k, v, og, ga, gb, g_t = _ffn_in_call(
        xs, mod_a[bp:], *ffn1, *in_w, nseq=32, t_tile=ts, act_dtype=F32)
    g_r = g_t.reshape(2 * N_HEADS, bs, ts).transpose(1, 0, 2)
    g_r = jnp.pad(g_r, ((0, 0), (0, 0), (0, LANES - ts)))
    m_b = jnp.broadcast_to(jnp.pad(st_m, ((0, 0), (0, LANES - N_HEADS)))[:, None, :], (bs, SUBLANES, LANES))
    hb, c_s, n_s, m_s = _mlstm_sample_call(q, k, v, og, g_r, st_c, st_n, m_b, mh, seq_block=4)
    y_s, conv_s = _mix_sample_call(x1, glu, hb, ga, gb, mod_b[bp:], st_conv, mix_w, nseq=16)

    return (y_p, y_s, conv_p, c_p, n_p, m_p[:, :N_HEADS, 0], conv_s, c_s, n_s, m_s[:, 0, :N_HEADS])


def kernel(x_prompt, x_sample, c_prompt, c_sample, state_conv, state_C, state_n, state_m, w_ada, b_ada, norm_ffn1, w1_gate, w1_up, w1_down, norm_mix, w_in, b_in, w_dw, b_dw, ln_conv_g, ln_conv_b, w_conv_out, mh_norm, w_mlstm_out, w_out, norm_ffn2, w2_gate, w2_up, w2_down, final_norm):
    depth = w_ada.shape[0]
    assert depth == 1, "the fused final norm assumes a single layer"
    c_all = jnp.concatenate([c_prompt, c_sample], axis=0)
    layer_w = (w_ada, b_ada, norm_ffn1, w1_gate, w1_up, w1_down, norm_mix, w_in, b_in, w_dw, b_dw,
               ln_conv_g, ln_conv_b, w_conv_out, mh_norm, w_mlstm_out, w_out, norm_ffn2, w2_gate, w2_up, w2_down)
    outs = _layer(x_prompt, x_sample, c_all, state_conv[0], state_C[0], state_n[0], state_m[0], final_norm,
                  *[w[0] for w in layer_w])
    y_p, y_s = outs[0], outs[1]
    return (y_p, y_s) + tuple(o[None] for o in outs[2:])
```

```python
import jax
import jax.numpy as jnp
from jax import lax
from jax.experimental import pallas as pl
from jax.experimental.pallas import tpu as pltpu

F32 = jnp.float32
BF16 = jnp.bfloat16

D_MODEL = 1024
N_HEADS = 4
HEAD_DIM = 256
CONV_WIDTH = 31
CONV_STATE = CONV_WIDTH - 1
D_FF = 2816
N_MOD = 9
CHUNK = 128
EPS = 1e-6

LANES = 128
SUBLANES = 8
CONV_PAD = 32
CONV_OFF = CONV_PAD - CONV_STATE
VMEM_LIMIT_BIG = 58 * 1024 * 1024
VMEM_LIMIT_SMALL = 40 * 1024 * 1024


def _sigmoid(x):
    return 1.0 / (1.0 + jnp.exp(-x))


def _log_sigmoid(x):
    return jnp.minimum(x, 0.0) - jnp.log1p(jnp.exp(-jnp.abs(x)))


def _rms(x, g):
    return x * lax.rsqrt(jnp.mean(x * x, axis=-1, keepdims=True) + EPS) * g


def _const_spec(shape):
    zeros = (0,) * len(shape)
    return pl.BlockSpec(shape, lambda *_: zeros, pipeline_mode=pl.Buffered(1))


def _mod_kernel(c_ref, w_ref, b_ref, o_ref):
    c = c_ref[...]
    a = (c * _sigmoid(c)).astype(BF16)
    o_ref[...] = jnp.dot(a, w_ref[...].astype(BF16), preferred_element_type=F32) + b_ref[...]


def _mod_call(c, w_ada, b_ada):
    rows = c.shape[0]
    n_out = w_ada.shape[1]
    bn = n_out // 8
    return pl.pallas_call(
        _mod_kernel,
        grid=(n_out // bn,),
        in_specs=[
            pl.BlockSpec((rows, D_MODEL), lambda j: (0, 0)),
            pl.BlockSpec((D_MODEL, bn), lambda j: (0, j)),
            pl.BlockSpec((1, bn), lambda j: (0, j)),
        ],
        out_specs=pl.BlockSpec((rows, bn), lambda j: (0, j)),
        out_shape=jax.ShapeDtypeStruct((rows, n_out), F32),
        compiler_params=pltpu.CompilerParams(
            dimension_semantics=("arbitrary",), vmem_limit_bytes=VMEM_LIMIT_SMALL),
        name="mod",
    )(c, w_ada, b_ada.reshape(1, n_out))


def _ffn_block(x3, nrm, sh, sc, gt, wg_ref, wu_ref, wd_ref):
    ns, t, d = x3.shape
    h = _rms(x3, nrm) * (1.0 + sc) + sh
    hb = h.reshape(ns * t, d).astype(BF16)
    g = jnp.dot(hb, wg_ref[...], preferred_element_type=F32)
    u = jnp.dot(hb, wu_ref[...], preferred_element_type=F32)
    a = (g * _sigmoid(g) * u).astype(BF16)
    dn = jnp.dot(a, wd_ref[...], preferred_element_type=F32)
    return x3 + (0.5 * gt) * dn.reshape(ns, t, d)


def _ffn_in_kernel(x_ref, mod_ref, n1_ref, wg_ref, wu_ref, wd_ref, n2_ref, wm_ref, bm_ref,
                   x1_ref, glu_ref, q_ref, k_ref, v_ref, og_ref, ga_ref, gb_ref, gt_ref):
    x = x_ref[...]
    ns, t, d = x.shape
    mod = mod_ref[...]
    sh1, sc1, g1, sh2, sc2 = [mod[:, i:i + 1, :] for i in range(5)]
    x1 = _ffn_block(x, n1_ref[...], sh1, sc1, g1, wg_ref, wu_ref, wd_ref)
    x1_ref[...] = x1
    u = _rms(x1, n2_ref[...]) * (1.0 + sc2) + sh2
    ub = u.reshape(ns * t, d).astype(BF16)

    def proj(j):
        cols = slice(j * d, (j + 1) * d)
        return jnp.dot(ub, wm_ref[:, cols], preferred_element_type=F32) + bm_ref[:, cols]

    def put(ref, val):
        ref[...] = val.reshape(ns, t, d).astype(ref.dtype)

    put(glu_ref, proj(0) * _sigmoid(proj(1)))
    put(q_ref, proj(2) * (HEAD_DIM ** -0.5))
    put(k_ref, proj(3))
    put(v_ref, proj(4))
    put(og_ref, _sigmoid(proj(5)))
    put(ga_ref, _sigmoid(proj(6)))
    put(gb_ref, _sigmoid(proj(7)))
    n_gates = gt_ref.shape[0]
    gates = jnp.dot(ub, wm_ref[:, 8 * d:], preferred_element_type=F32) + bm_ref[:, 8 * d:]
    gt_ref[...] = gates.T[0:n_gates, :]


def _ffn_in_call(x, mod_a, nrm1, wg, wu, wd, nrm2, w_main, b_main, *, nseq, t_tile, act_dtype):
    b, s, d = x.shape
    grid = (b // nseq, s // t_tile)
    rows = nseq * t_tile
    tok = pl.BlockSpec((nseq, t_tile, d), lambda i, j: (i, j, 0))
    n_t = s // t_tile
    in_specs = [
        tok,
        pl.BlockSpec((nseq, 5, d), lambda i, j: (i, 0, 0)),
        _const_spec((1, d)),
        _const_spec(wg.shape), _const_spec(wu.shape), _const_spec(wd.shape),
        _const_spec((1, d)),
        _const_spec(w_main.shape), _const_spec(b_main.shape),
    ]
    out_specs = [tok] * 8 + [pl.BlockSpec((2 * N_HEADS, rows), lambda i, j: (0, i * n_t + j))]
    out_shape = ([jax.ShapeDtypeStruct((b, s, d), F32)] * 2
                 + [jax.ShapeDtypeStruct((b, s, d), act_dtype)] * 6
                 + [jax.ShapeDtypeStruct((2 * N_HEADS, b * s), F32)])
    return pl.pallas_call(
        _ffn_in_kernel,
        grid=grid,
        in_specs=in_specs,
        out_specs=out_specs,
        out_shape=out_shape,
        compiler_params=pltpu.CompilerParams(
            dimension_semantics=("arbitrary", "arbitrary"), vmem_limit_bytes=VMEM_LIMIT_BIG),
        name="ffn_in",
    )(x, mod_a, nrm1, wg, wu, wd, nrm2, w_main, b_main)


def _seg_cumsum_lanes(x, seg):
    lane = lax.broadcasted_iota(jnp.int32, x.shape, 1)
    pos = lane & (LANES - 1)
    sh = 1
    while sh < seg:
        x = x + jnp.where(pos >= sh, pltpu.roll(x, sh, 1), 0.0)
        sh *= 2
    return x


def _mlstm_head_chunk(q, k, v, g_row, lf_row, c_st, n_row, m):
    l = q.shape[0]
    t_idx = lax.broadcasted_iota(jnp.int32, (l, l), 0)
    s_idx = lax.broadcasted_iota(jnp.int32, (l, l), 1)
    causal = s_idx <= t_idx
    gb = jnp.broadcast_to(g_row, (l, l))
    lfb = jnp.broadcast_to(lf_row, (l, l))
    cm_col = jnp.max(jnp.where(causal, gb, -jnp.inf), axis=1, keepdims=True)
    bcum_col = jnp.sum(jnp.where(causal, lfb, 0.0), axis=1, keepdims=True)
    g_col = jnp.sum(jnp.where(s_idx == t_idx, gb, 0.0), axis=1, keepdims=True)
    g_max = cm_col[l - 1:l, :]
    qb, kb, vb = q.astype(BF16), k.astype(BF16), v.astype(BF16)

    s = lax.dot_general(qb, kb, (((1,), (1,)), ((), ())), preferred_element_type=F32)
    p = (s * jnp.where(causal, jnp.exp(gb - cm_col), 0.0)).astype(BF16)
    pv = jnp.dot(p, vb, preferred_element_type=F32)
    den_loc = jnp.dot(p, jnp.ones((l, LANES), BF16), preferred_element_type=F32)
    ws_loc = jnp.exp(g_col - g_max)
    vs = (v.astype(F32) * ws_loc).astype(BF16)
    upd = lax.dot_general(vs, kb, (((0,), (0,)), ((), ())), preferred_element_type=F32)
    n_upd = jnp.sum(ws_loc * k.astype(F32), axis=0, keepdims=True)

    m_col = jnp.maximum(m, cm_col)
    r_col = jnp.exp(cm_col - m_col)
    w_inter = jnp.exp(m - m_col)
    qc = lax.dot_general(qb, c_st.astype(BF16), (((1,), (1,)), ((), ())), preferred_element_type=F32)
    n_rep = jnp.broadcast_to(n_row, (LANES, n_row.shape[1])).astype(BF16)
    qn = lax.dot_general(qb, n_rep, (((1,), (1,)), ((), ())), preferred_element_type=F32)
    num = r_col * pv + w_inter * qc
    den = jnp.maximum(jnp.abs(r_col * den_loc + w_inter * qn), jnp.exp(-(bcum_col + m_col)))
    inv_den = 1.0 / den
    h = num * jnp.concatenate([inv_den] * (num.shape[1] // LANES), axis=1)
    m_last = m_col[l - 1:l, :]
    a_prev = jnp.exp(m - m_last)
    scale = jnp.exp(g_max - m_last)
    c_new = a_prev * c_st + scale * upd
    n_new = a_prev * n_row + scale * n_upd
    m_new = bcum_col[l - 1:l, :] + m_last
    return h, c_new, n_new, m_new


def _head_out(h, og, mh):
    hn = h * lax.rsqrt(jnp.mean(h * h, axis=-1, keepdims=True) + EPS) * mh
    return og.astype(F32) * hn


CONV_ROW_BLOCK = 8 * SUBLANES


def _conv_prompt_tile(cbuf, xs, wdw_ref, bdw_ref, ycv_ref):
    n_rows, d = cbuf.shape
    t = n_rows - CONV_PAD
    span = n_rows - SUBLANES
    for lb in range(d // LANES):
        ls = slice(lb * LANES, (lb + 1) * LANES)
        xl = xs.at[lb % 2]
        for r in range(1, SUBLANES):
            xl[r, 0:span, :] = cbuf[r:r + span, ls]
        w_l = wdw_ref[:, ls]
        b_l = bdw_ref[:, ls]
        for r0 in range(0, t, CONV_ROW_BLOCK):
            acc = jnp.broadcast_to(b_l, (CONV_ROW_BLOCK, LANES))
            for j in range(CONV_WIDTH):
                r = (CONV_OFF + j) % SUBLANES
                lo = CONV_OFF + j - r + r0
                src = cbuf[lo:lo + CONV_ROW_BLOCK, ls] if r == 0 else xl[r, lo:lo + CONV_ROW_BLOCK, :]
                acc = acc + w_l[j:j + 1, :] * src
            ycv_ref[0, r0:r0 + CONV_ROW_BLOCK, ls] = acc


def _seqmix_prompt_kernel(q_ref, k_ref, v_ref, og_ref, gt_ref, glu_ref, mh_ref, wdw_ref, bdw_ref,
                          hb_ref, ycv_ref, c_out, n_out, m_out, cso_ref, c_s, n_s, m_s, cbuf, xs):
    t = pl.program_id(1)
    tm = q_ref.shape[1]

    @pl.when(t == 0)
    def _():
        c_s[...] = jnp.zeros_like(c_s)
        n_s[...] = jnp.zeros_like(n_s)
        m_s[...] = jnp.zeros_like(m_s)
        cbuf[0:CONV_PAD, :] = jnp.zeros((CONV_PAD, cbuf.shape[1]), F32)

    @pl.when(t > 0)
    def _():
        cbuf[0:CONV_PAD, :] = cbuf[tm:tm + CONV_PAD, :]

    cbuf[CONV_PAD:CONV_PAD + tm, :] = glu_ref[0]
    _conv_prompt_tile(cbuf, xs, wdw_ref, bdw_ref, ycv_ref)
    cso_ref[0] = cbuf[CONV_OFF + tm:CONV_PAD + tm, :]

    g8 = gt_ref[...]
    lf8 = _log_sigmoid(g8)
    bc8 = _seg_cumsum_lanes(lf8, CHUNK)
    lf = lf8[N_HEADS:2 * N_HEADS, :]
    gg = g8[0:N_HEADS, :] - bc8[N_HEADS:2 * N_HEADS, :]
    for c in range(tm // CHUNK):
        rows = slice(c * CHUNK, (c + 1) * CHUNK)
        for hd in range(N_HEADS):
            cols = slice(hd * HEAD_DIM, (hd + 1) * HEAD_DIM)
            h, c_new, n_new, m_new = _mlstm_head_chunk(
                q_ref[0, rows, cols], k_ref[0, rows, cols], v_ref[0, rows, cols],
                gg[hd:hd + 1, rows], lf[hd:hd + 1, rows],
                c_s[hd], n_s[hd:hd + 1, :], m_s[hd:hd + 1, 0:1])
            c_s[hd] = c_new
            n_s[hd:hd + 1, :] = n_new
            m_s[hd:hd + 1, :] = jnp.broadcast_to(m_new, (1, LANES))
            hb_ref[0, rows, cols] = _head_out(h, og_ref[0, rows, cols], mh_ref[:, cols]).astype(hb_ref.dtype)

    @pl.when(t == pl.num_programs(1) - 1)
    def _():
        c_out[0] = c_s[...]
        n_out[0] = n_s[0:N_HEADS, :]
        m_out[0] = m_s[...]


def _seqmix_prompt_call(q, k, v, og, g_t, glu, mh, w_dw, b_dw, *, t_tile):
    b, s, d = q.shape
    n_t = s // t_tile
    tok = pl.BlockSpec((1, t_tile, d), lambda i, j: (i, j, 0))
    return pl.pallas_call(
        _seqmix_prompt_kernel,
        grid=(b, n_t),
        in_specs=[tok, tok, tok, tok,
                  pl.BlockSpec((2 * N_HEADS, t_tile), lambda i, j: (0, i * n_t + j)),
                  tok, _const_spec((1, d)), _const_spec(w_dw.shape), _const_spec((1, d))],
        out_specs=[tok, tok,
                   pl.BlockSpec((1, N_HEADS, HEAD_DIM, HEAD_DIM), lambda i, j: (i, 0, 0, 0)),
                   pl.BlockSpec((1, N_HEADS, HEAD_DIM), lambda i, j: (i, 0, 0)),
                   pl.BlockSpec((1, SUBLANES, LANES), lambda i, j: (i, 0, 0)),
                   pl.BlockSpec((None, 1, CONV_STATE, d), lambda i, j: (0, i, 0, 0))],
        out_shape=[jax.ShapeDtypeStruct((b, s, d), BF16),
                   jax.ShapeDtypeStruct((b, s, d), F32),
                   jax.ShapeDtypeStruct((b, N_HEADS, HEAD_DIM, HEAD_DIM), F32),
                   jax.ShapeDtypeStruct((b, N_HEADS, HEAD_DIM), F32),
                   jax.ShapeDtypeStruct((b, SUBLANES, LANES), F32),
                   jax.ShapeDtypeStruct((1, b, CONV_STATE, d), F32)],
        scratch_shapes=[pltpu.VMEM((N_HEADS, HEAD_DIM, HEAD_DIM), F32),
                        pltpu.VMEM((SUBLANES, HEAD_DIM), F32),
                        pltpu.VMEM((SUBLANES, LANES), F32),
                        pltpu.VMEM((CONV_PAD + t_tile, d), F32),
                        pltpu.VMEM((2, SUBLANES, CONV_PAD + t_tile, LANES), F32)],
        compiler_params=pltpu.CompilerParams(
            dimension_semantics=("arbitrary", "arbitrary"), vmem_limit_bytes=VMEM_LIMIT_SMALL),
        name="seqmix_prompt",
    )(q, k, v, og, g_t, glu, mh, w_dw, b_dw)


def _seqmix_sample_kernel(q_ref, k_ref, v_ref, og_ref, gr_ref, c_in, n_in, m_in, glu_ref, cst_ref,
                          mh_ref, wdw_ref, bdw_ref,
                          hb_ref, ycv_ref, c_out, n_out, m_out, cso_ref, cbuf):
    sb, t, d = q_ref.shape
    cbuf[:, CONV_OFF:CONV_PAD, :] = cst_ref[...]
    cbuf[:, CONV_PAD:CONV_PAD + t, :] = glu_ref[...]
    for lb in range(d // LANES):
        ls = slice(lb * LANES, (lb + 1) * LANES)
        w_l = wdw_ref[:, ls]
        acc = jnp.broadcast_to(bdw_ref[:, ls], (sb, t, LANES))
        for j in range(CONV_WIDTH):
            acc = acc + w_l[j:j + 1, :] * cbuf[:, CONV_OFF + j:CONV_OFF + j + t, ls]
        ycv_ref[:, :, ls] = acc
    cso_ref[...] = cbuf[:, CONV_OFF + t:CONV_PAD + t, :]

    lane = lax.broadcasted_iota(jnp.int32, (SUBLANES, LANES), 1)
    for s in range(sb):
        g8 = gr_ref[s]
        lf8 = _log_sigmoid(g8)
        bc8 = _seg_cumsum_lanes(lf8, t)
        lf = lf8[N_HEADS:2 * N_HEADS, :]
        gg = g8[0:N_HEADS, :] - bc8[N_HEADS:2 * N_HEADS, :]
        m_blk = m_in[s]
        m_acc = jnp.zeros((SUBLANES, LANES), F32)
        for hd in range(N_HEADS):
            cols = slice(hd * HEAD_DIM, (hd + 1) * HEAD_DIM)
            h, c_new, n_new, m_new = _mlstm_head_chunk(
                q_ref[s, :, cols], k_ref[s, :, cols], v_ref[s, :, cols],
                gg[hd:hd + 1, 0:t], lf[hd:hd + 1, 0:t],
                c_in[s, hd], n_in[s, hd:hd + 1, :], m_blk[0:1, hd:hd + 1])
            c_out[s, hd] = c_new
            n_out[s, hd:hd + 1, :] = n_new
            m_acc = jnp.where(lane == hd, m_new, m_acc)
            hb_ref[s, :, cols] = _head_out(h, og_ref[s, :, cols], mh_ref[:, cols]).astype(hb_ref.dtype)
        m_out[s] = m_acc


def _seqmix_sample_call(q, k, v, og, g_r, c0, n0, m0, glu, conv_state, mh, w_dw, b_dw, *, seq_block):
    b, t, d = q.shape
    tok = pl.BlockSpec((seq_block, t, d), lambda i: (i, 0, 0))
    small = pl.BlockSpec((seq_block, SUBLANES, LANES), lambda i: (i, 0, 0))
    c_spec = pl.BlockSpec((seq_block, N_HEADS, HEAD_DIM, HEAD_DIM), lambda i: (i, 0, 0, 0))
    n_spec = pl.BlockSpec((seq_block, N_HEADS, HEAD_DIM), lambda i: (i, 0, 0))
    st_spec = pl.BlockSpec((None, seq_block, CONV_STATE, d), lambda i: (0, i, 0, 0))
    return pl.pallas_call(
        _seqmix_sample_kernel,
        grid=(b // seq_block,),
        in_specs=[tok, tok, tok, tok, small, c_spec, n_spec, small, tok, st_spec,
                  _const_spec((1, d)), _const_spec(w_dw.shape), _const_spec((1, d))],
        out_specs=[tok, tok, c_spec, n_spec, small, st_spec],
        out_shape=[jax.ShapeDtypeStruct((b, t, d), F32),
                   jax.ShapeDtypeStruct((b, t, d), F32),
                   jax.ShapeDtypeStruct(c0.shape, F32),
                   jax.ShapeDtypeStruct(n0.shape, F32),
                   jax.ShapeDtypeStruct((b, SUBLANES, LANES), F32),
                   jax.ShapeDtypeStruct((1, b, CONV_STATE, d), F32)],
        scratch_shapes=[pltpu.VMEM((seq_block, CONV_PAD + t, d), F32)],
        compiler_params=pltpu.CompilerParams(
            dimension_semantics=("arbitrary",), vmem_limit_bytes=VMEM_LIMIT_SMALL),
        name="seqmix_sample",
    )(q, k, v, og, g_r, c0, n0, m0, glu, conv_state, mh, w_dw, b_dw)


def _mix_tail(x1, yc3, hb, ga, gb, mod, lng_ref, lnb_ref, wco_ref, wmo_ref, wo_ref,
              n3_ref, wg_ref, wu_ref, wd_ref, fn_ref):
    ns, t, d = x1.shape
    rows = ns * t
    yc = yc3.reshape(rows, d)
    mu = jnp.mean(yc, axis=-1, keepdims=True)
    yd = yc - mu
    var = jnp.mean(yd * yd, axis=-1, keepdims=True)
    ln = yd * lax.rsqrt(var + EPS) * lng_ref[...] + lnb_ref[...]
    a_in = (ln * _sigmoid(ln)).astype(BF16)
    a_out = jnp.dot(a_in, wco_ref[...], preferred_element_type=F32)
    b_out = jnp.dot(hb.reshape(rows, d).astype(BF16), wmo_ref[...], preferred_element_type=F32)
    z = (ga.reshape(rows, d).astype(F32) * a_out + gb.reshape(rows, d).astype(F32) * b_out).astype(BF16)
    zo = jnp.dot(z, wo_ref[...], preferred_element_type=F32)
    g2, sh3, sc3, g3 = [mod[:, i:i + 1, :] for i in range(4)]
    x2 = x1 + g2 * zo.reshape(ns, t, d)
    x3 = _ffn_block(x2, n3_ref[...], sh3, sc3, g3, wg_ref, wu_ref, wd_ref)
    return _rms(x3, fn_ref[...])


def _tail_kernel(x1_ref, ycv_ref, hb_ref, ga_ref, gb_ref, mod_ref, *rest):
    *tail_w, y_ref = rest
    y_ref[...] = _mix_tail(x1_ref[...], ycv_ref[...], hb_ref[...], ga_ref[...], gb_ref[...], mod_ref[...], *tail_w)


def _tail_call(x1, ycv, hb, ga, gb, mod_b, weights, *, nseq, t_tile):
    b, s, d = x1.shape
    tok = pl.BlockSpec((nseq, t_tile, d), lambda i, j: (i, j, 0))
    in_specs = ([tok] * 5 + [pl.BlockSpec((nseq, 4, d), lambda i, j: (i, 0, 0))]
                + [_const_spec(w.shape) for w in weights])
    return pl.pallas_call(
        _tail_kernel,
        grid=(b // nseq, s // t_tile),
        in_specs=in_specs,
        out_specs=tok,
        out_shape=jax.ShapeDtypeStruct((b, s, d), F32),
        compiler_params=pltpu.CompilerParams(
            dimension_semantics=("arbitrary", "arbitrary"), vmem_limit_bytes=VMEM_LIMIT_BIG),
        name="tail",
    )(x1, ycv, hb, ga, gb, mod_b, *weights)


def _layer(xp, xs, c_all, st_conv, st_c, st_n, st_m, final_norm,
           w_ada, b_ada, norm_ffn1, w1_gate, w1_up, w1_down, norm_mix, w_in, b_in, w_dw, b_dw,
           ln_conv_g, ln_conv_b, w_conv_out, mh_norm, w_mlstm_out, w_out, norm_ffn2, w2_gate, w2_up, w2_down):
    bp = xp.shape[0]
    bs, ts, d = xs.shape
    row = lambda a: a.reshape(1, -1)

    mod = _mod_call(c_all, w_ada, b_ada).reshape(-1, N_MOD, d)
    mod_a, mod_b = mod[:, 0:5, :], mod[:, 5:N_MOD, :]

    n_in = w_in.shape[1]
    assert n_in == 8 * d + 2 * N_HEADS
    col_pad = -n_in % LANES
    w_main = jnp.pad(w_in, ((0, 0), (0, col_pad))).astype(BF16)
    b_main = row(jnp.pad(b_in, (0, col_pad)))
    ffn1 = (row(norm_ffn1), w1_gate.astype(BF16), w1_up.astype(BF16), w1_down.astype(BF16))
    in_w = (row(norm_mix), w_main, b_main)
    conv_w = (row(mh_norm), jnp.concatenate([w_dw, jnp.zeros((CONV_PAD - CONV_WIDTH, d), F32)], axis=0), row(b_dw))
    tail_w = (row(ln_conv_g), row(ln_conv_b),
              w_conv_out.astype(BF16), w_mlstm_out.astype(BF16), w_out.astype(BF16),
              row(norm_ffn2), w2_gate.astype(BF16), w2_up.astype(BF16), w2_down.astype(BF16), row(final_norm))

    x1, glu, q, k, v, og, ga, gb, g_t = _ffn_in_call(
        xp, mod_a[:bp], *ffn1, *in_w, nseq=1, t_tile=256, act_dtype=BF16)
    hb, ycv, c_p, n_p, m_p, conv_p = _seqmix_prompt_call(q, k, v, og, g_t, glu, *conv_w, t_tile=512)
    y_p = _tail_call(x1, ycv, hb, ga, gb, mod_b[:bp], tail_w, nseq=1, t_tile=256)

    x1, glu, q, k, v, og, ga, gb, g_t = _ffn_in_call(
        xs, mod_a[bp:], *ffn1, *in_w, nseq=32, t_tile=ts, act_dtype=F32)
    g_r = g_t.reshape(2 * N_HEADS, bs, ts).transpose(1, 0, 2)
    g_r = jnp.pad(g_r, ((0, 0), (0, 0), (0, LANES - ts)))
    m_b = jnp.broadcast_to(jnp.pad(st_m, ((0, 0), (0, LANES - N_HEADS)))[:, None, :], (bs, SUBLANES, LANES))
    hb, ycv, c_s, n_s, m_s, conv_s = _seqmix_sample_call(
        q, k, v, og, g_r, st_c, st_n, m_b, glu, st_conv, *conv_w, seq_block=4)
    y_s = _tail_call(x1, ycv, hb, ga, gb, mod_b[bp:], tail_w, nseq=32, t_tile=ts)

    return (y_p, y_s, conv_p, c_p, n_p, m_p[:, :N_HEADS, 0], conv_s, c_s, n_s, m_s[:, 0, :N_HEADS])


def kernel(x_prompt, x_sample, c_prompt, c_sample, state_conv, state_C, state_n, state_m, w_ada, b_ada, norm_ffn1, w1_gate, w1_up, w1_down, norm_mix, w_in, b_in, w_dw, b_dw, ln_conv_g, ln_conv_b, w_conv_out, mh_norm, w_mlstm_out, w_out, norm_ffn2, w2_gate, w2_up, w2_down, final_norm):
    depth = w_ada.shape[0]
    assert depth == 1, "the fused final norm assumes a single layer"
    c_all = jnp.concatenate([c_prompt, c_sample], axis=0)
    layer_w = (w_ada, b_ada, norm_ffn1, w1_gate, w1_up, w1_down, norm_mix, w_in, b_in, w_dw, b_dw,
               ln_conv_g, ln_conv_b, w_conv_out, mh_norm, w_mlstm_out, w_out, norm_ffn2, w2_gate, w2_up, w2_down)
    outs = _layer(x_prompt, x_sample, c_all, state_conv, state_C[0], state_n[0], state_m[0], final_norm,
                  *[w[0] for w in layer_w])
    conv_idx = (2, 6)
    return tuple(o if i < 2 or i in conv_idx else o[None] for i, o in enumerate(outs))
```

```python
import jax
import jax.numpy as jnp
from jax import lax
from jax.experimental import pallas as pl
from jax.experimental.pallas import tpu as pltpu

F32 = jnp.float32
BF16 = jnp.bfloat16

D_MODEL = 1024
N_HEADS = 4
HEAD_DIM = 256
CONV_WIDTH = 31
CONV_STATE = CONV_WIDTH - 1
D_FF = 2816
N_MOD = 9
CHUNK = 128
EPS = 1e-6

LANES = 128
SUBLANES = 8
CONV_PAD = 32
CONV_OFF = CONV_PAD - CONV_STATE
VMEM_LIMIT_BIG = 58 * 1024 * 1024
VMEM_LIMIT_SMALL = 40 * 1024 * 1024


def _sigmoid(x):
    return 1.0 / (1.0 + jnp.exp(-x))


def _log_sigmoid(x):
    return jnp.minimum(x, 0.0) - jnp.log1p(jnp.exp(-jnp.abs(x)))


def _rms(x, g):
    return x * lax.rsqrt(jnp.mean(x * x, axis=-1, keepdims=True) + EPS) * g


def _const_spec(shape):
    zeros = (0,) * len(shape)
    return pl.BlockSpec(shape, lambda *_: zeros, pipeline_mode=pl.Buffered(1))


def _mod_kernel(c_ref, w_ref, b_ref, o_ref):
    c = c_ref[...]
    a = (c * _sigmoid(c)).astype(BF16)
    o_ref[...] = jnp.dot(a, w_ref[...].astype(BF16), preferred_element_type=F32) + b_ref[...]


def _mod_call(c, w_ada, b_ada):
    rows = c.shape[0]
    n_out = w_ada.shape[1]
    bn = n_out // 8
    return pl.pallas_call(
        _mod_kernel,
        grid=(n_out // bn,),
        in_specs=[
            pl.BlockSpec((rows, D_MODEL), lambda j: (0, 0)),
            pl.BlockSpec((D_MODEL, bn), lambda j: (0, j)),
            pl.BlockSpec((1, bn), lambda j: (0, j)),
        ],
        out_specs=pl.BlockSpec((rows, bn), lambda j: (0, j)),
        out_shape=jax.ShapeDtypeStruct((rows, n_out), F32),
        compiler_params=pltpu.CompilerParams(
            dimension_semantics=("arbitrary",), vmem_limit_bytes=VMEM_LIMIT_SMALL),
        name="mod",
    )(c, w_ada, b_ada.reshape(1, n_out))


def _ffn_block(x3, nrm, sh, sc, gt, wg_ref, wu_ref, wd_ref):
    ns, t, d = x3.shape
    h = _rms(x3, nrm) * (1.0 + sc) + sh
    hb = h.reshape(ns * t, d).astype(BF16)
    g = jnp.dot(hb, wg_ref[...], preferred_element_type=F32)
    u = jnp.dot(hb, wu_ref[...], preferred_element_type=F32)
    a = (g * _sigmoid(g) * u).astype(BF16)
    dn = jnp.dot(a, wd_ref[...], preferred_element_type=F32)
    return x3 + (0.5 * gt) * dn.reshape(ns, t, d)


def _ffn_in_kernel(x_ref, mod_ref, n1_ref, wg_ref, wu_ref, wd_ref, n2_ref, wm_ref, bm_ref,
                   x1_ref, glu_ref, q_ref, k_ref, v_ref, og_ref, ga_ref, gb_ref, gt_ref):
    x = x_ref[...]
    ns, t, d = x.shape
    mod = mod_ref[...]
    sh1, sc1, g1, sh2, sc2 = [mod[:, i:i + 1, :] for i in range(5)]
    x1 = _ffn_block(x, n1_ref[...], sh1, sc1, g1, wg_ref, wu_ref, wd_ref)
    x1_ref[...] = x1
    u = _rms(x1, n2_ref[...]) * (1.0 + sc2) + sh2
    ub = u.reshape(ns * t, d).astype(BF16)

    def proj(j):
        cols = slice(j * d, (j + 1) * d)
        return jnp.dot(ub, wm_ref[:, cols], preferred_element_type=F32) + bm_ref[:, cols]

    def put(ref, val):
        ref[...] = val.reshape(ns, t, d).astype(ref.dtype)

    put(glu_ref, proj(0) * _sigmoid(proj(1)))
    put(q_ref, proj(2) * (HEAD_DIM ** -0.5))
    put(k_ref, proj(3))
    put(v_ref, proj(4))
    put(og_ref, _sigmoid(proj(5)))
    put(ga_ref, _sigmoid(proj(6)))
    put(gb_ref, _sigmoid(proj(7)))
    n_gates = gt_ref.shape[0]
    gates = jnp.dot(ub, wm_ref[:, 8 * d:], preferred_element_type=F32) + bm_ref[:, 8 * d:]
    gt_ref[...] = gates.T[0:n_gates, :]


def _ffn_in_call(x, mod_a, nrm1, wg, wu, wd, nrm2, w_main, b_main, *, nseq, t_tile, act_dtype):
    b, s, d = x.shape
    grid = (b // nseq, s // t_tile)
    rows = nseq * t_tile
    tok = pl.BlockSpec((nseq, t_tile, d), lambda i, j: (i, j, 0))
    n_t = s // t_tile
    in_specs = [
        tok,
        pl.BlockSpec((nseq, 5, d), lambda i, j: (i, 0, 0)),
        _const_spec((1, d)),
        _const_spec(wg.shape), _const_spec(wu.shape), _const_spec(wd.shape),
        _const_spec((1, d)),
        _const_spec(w_main.shape), _const_spec(b_main.shape),
    ]
    out_specs = [tok] * 8 + [pl.BlockSpec((2 * N_HEADS, rows), lambda i, j: (0, i * n_t + j))]
    out_shape = ([jax.ShapeDtypeStruct((b, s, d), F32)] * 2
                 + [jax.ShapeDtypeStruct((b, s, d), act_dtype)] * 6
                 + [jax.ShapeDtypeStruct((2 * N_HEADS, b * s), F32)])
    return pl.pallas_call(
        _ffn_in_kernel,
        grid=grid,
        in_specs=in_specs,
        out_specs=out_specs,
        out_shape=out_shape,
        compiler_params=pltpu.CompilerParams(
            dimension_semantics=("arbitrary", "arbitrary"), vmem_limit_bytes=VMEM_LIMIT_BIG),
        name="ffn_in",
    )(x, mod_a, nrm1, wg, wu, wd, nrm2, w_main, b_main)


def _seg_cumsum_lanes(x, seg):
    lane = lax.broadcasted_iota(jnp.int32, x.shape, 1)
    pos = lane & (LANES - 1)
    sh = 1
    while sh < seg:
        x = x + jnp.where(pos >= sh, pltpu.roll(x, sh, 1), 0.0)
        sh *= 2
    return x


def _mlstm_head_chunk(q, k, v, g_row, lf_row, c_st, n_row, m):
    l = q.shape[0]
    t_idx = lax.broadcasted_iota(jnp.int32, (l, l), 0)
    s_idx = lax.broadcasted_iota(jnp.int32, (l, l), 1)
    causal = s_idx <= t_idx
    gb = jnp.broadcast_to(g_row, (l, l))
    lfb = jnp.broadcast_to(lf_row, (l, l))
    cm_col = jnp.max(jnp.where(causal, gb, -jnp.inf), axis=1, keepdims=True)
    bcum_col = jnp.sum(jnp.where(causal, lfb, 0.0), axis=1, keepdims=True)
    g_col = jnp.sum(jnp.where(s_idx == t_idx, gb, 0.0), axis=1, keepdims=True)
    g_max = cm_col[l - 1:l, :]
    qb, kb, vb = q.astype(BF16), k.astype(BF16), v.astype(BF16)

    s = lax.dot_general(qb, kb, (((1,), (1,)), ((), ())), preferred_element_type=F32)
    p = (s * jnp.where(causal, jnp.exp(gb - cm_col), 0.0)).astype(BF16)
    pv = jnp.dot(p, vb, preferred_element_type=F32)
    den_loc = jnp.dot(p, jnp.ones((l, LANES), BF16), preferred_element_type=F32)
    ws_loc = jnp.exp(g_col - g_max)
    vs = (v.astype(F32) * ws_loc).astype(BF16)
    upd = lax.dot_general(vs, kb, (((0,), (0,)), ((), ())), preferred_element_type=F32)
    n_upd = jnp.sum(ws_loc * k.astype(F32), axis=0, keepdims=True)

    m_col = jnp.maximum(m, cm_col)
    r_col = jnp.exp(cm_col - m_col)
    w_inter = jnp.exp(m - m_col)
    qc = lax.dot_general(qb, c_st.astype(BF16), (((1,), (1,)), ((), ())), preferred_element_type=F32)
    n_rep = jnp.broadcast_to(n_row, (LANES, n_row.shape[1])).astype(BF16)
    qn = lax.dot_general(qb, n_rep, (((1,), (1,)), ((), ())), preferred_element_type=F32)
    num = r_col * pv + w_inter * qc
    den = jnp.maximum(jnp.abs(r_col * den_loc + w_inter * qn), jnp.exp(-(bcum_col + m_col)))
    inv_den = 1.0 / den
    h = num * jnp.concatenate([inv_den] * (num.shape[1] // LANES), axis=1)
    m_last = m_col[l - 1:l, :]
    a_prev = jnp.exp(m - m_last)
    scale = jnp.exp(g_max - m_last)
    c_new = a_prev * c_st + scale * upd
    n_new = a_prev * n_row + scale * n_upd
    m_new = bcum_col[l - 1:l, :] + m_last
    return h, c_new, n_new, m_new


def _head_out(h, og, mh):
    hn = h * lax.rsqrt(jnp.mean(h * h, axis=-1, keepdims=True) + EPS) * mh
    return og.astype(F32) * hn


CONV_ROW_BLOCK = 8 * SUBLANES


def _conv_prompt_tile(cbuf, xs, wdw_ref, bdw_ref, ycv_ref):
    n_rows, d = cbuf.shape
    t = n_rows - CONV_PAD
    span = n_rows - SUBLANES
    for lb in range(d // LANES):
        ls = slice(lb * LANES, (lb + 1) * LANES)
        xl = xs.at[lb % 2]
        for r in range(1, SUBLANES):
            xl[r, 0:span, :] = cbuf[r:r + span, ls]
        w_l = wdw_ref[:, ls]
        b_l = bdw_ref[:, ls]
        for r0 in range(0, t, CONV_ROW_BLOCK):
            acc = jnp.broadcast_to(b_l, (CONV_ROW_BLOCK, LANES))
            for j in range(CONV_WIDTH):
                r = (CONV_OFF + j) % SUBLANES
                lo = CONV_OFF + j - r + r0
                src = cbuf[lo:lo + CONV_ROW_BLOCK, ls] if r == 0 else xl[r, lo:lo + CONV_ROW_BLOCK, :]
                acc = acc + w_l[j:j + 1, :] * src
            ycv_ref[0, r0:r0 + CONV_ROW_BLOCK, ls] = acc


def _seqmix_prompt_kernel(q_ref, k_ref, v_ref, og_ref, gt_ref, glu_ref, mh_ref, wdw_ref, bdw_ref,
                          hb_ref, ycv_ref, c_out, n_out, m_out, cso_ref, c_s, n_s, m_s, cbuf, xs):
    t = pl.program_id(1)
    tm = q_ref.shape[1]

    @pl.when(t == 0)
    def _():
        c_s[...] = jnp.zeros_like(c_s)
        n_s[...] = jnp.zeros_like(n_s)
        m_s[...] = jnp.zeros_like(m_s)
        cbuf[0:CONV_PAD, :] = jnp.zeros((CONV_PAD, cbuf.shape[1]), F32)

    @pl.when(t > 0)
    def _():
        cbuf[0:CONV_PAD, :] = cbuf[tm:tm + CONV_PAD, :]

    cbuf[CONV_PAD:CONV_PAD + tm, :] = glu_ref[0]
    _conv_prompt_tile(cbuf, xs, wdw_ref, bdw_ref, ycv_ref)
    cso_ref[0] = cbuf[CONV_OFF + tm:CONV_PAD + tm, :]

    g8 = gt_ref[...]
    lf8 = _log_sigmoid(g8)
    bc8 = _seg_cumsum_lanes(lf8, CHUNK)
    lf = lf8[N_HEADS:2 * N_HEADS, :]
    gg = g8[0:N_HEADS, :] - bc8[N_HEADS:2 * N_HEADS, :]
    for c in range(tm // CHUNK):
        rows = slice(c * CHUNK, (c + 1) * CHUNK)
        for hd in range(N_HEADS):
            cols = slice(hd * HEAD_DIM, (hd + 1) * HEAD_DIM)
            h, c_new, n_new, m_new = _mlstm_head_chunk(
                q_ref[0, rows, cols], k_ref[0, rows, cols], v_ref[0, rows, cols],
                gg[hd:hd + 1, rows], lf[hd:hd + 1, rows],
                c_s[hd], n_s[hd:hd + 1, :], m_s[hd:hd + 1, 0:1])
            c_s[hd] = c_new
            n_s[hd:hd + 1, :] = n_new
            m_s[hd:hd + 1, :] = jnp.broadcast_to(m_new, (1, LANES))
            hb_ref[0, rows, cols] = _head_out(h, og_ref[0, rows, cols], mh_ref[:, cols]).astype(hb_ref.dtype)

    @pl.when(t == pl.num_programs(1) - 1)
    def _():
        c_out[0] = c_s[...]
        n_out[0] = n_s[0:N_HEADS, :]
        m_out[0] = m_s[...]


def _seqmix_prompt_call(q, k, v, og, g_t, glu, mh, w_dw, b_dw, *, t_tile):
    b, s, d = q.shape
    n_t = s // t_tile
    tok = pl.BlockSpec((1, t_tile, d), lambda i, j: (i, j, 0))
    return pl.pallas_call(
        _seqmix_prompt_kernel,
        grid=(b, n_t),
        in_specs=[tok, tok, tok, tok,
                  pl.BlockSpec((2 * N_HEADS, t_tile), lambda i, j: (0, i * n_t + j)),
                  tok, _const_spec((1, d)), _const_spec(w_dw.shape), _const_spec((1, d))],
        out_specs=[tok, tok,
                   pl.BlockSpec((1, N_HEADS, HEAD_DIM, HEAD_DIM), lambda i, j: (i, 0, 0, 0)),
                   pl.BlockSpec((1, N_HEADS, HEAD_DIM), lambda i, j: (i, 0, 0)),
                   pl.BlockSpec((1, SUBLANES, LANES), lambda i, j: (i, 0, 0)),
                   pl.BlockSpec((None, 1, CONV_STATE, d), lambda i, j: (0, i, 0, 0))],
        out_shape=[jax.ShapeDtypeStruct((b, s, d), BF16),
                   jax.ShapeDtypeStruct((b, s, d), F32),
                   jax.ShapeDtypeStruct((b, N_HEADS, HEAD_DIM, HEAD_DIM), F32),
                   jax.ShapeDtypeStruct((b, N_HEADS, HEAD_DIM), F32),
                   jax.ShapeDtypeStruct((b, SUBLANES, LANES), F32),
                   jax.ShapeDtypeStruct((1, b, CONV_STATE, d), F32)],
        scratch_shapes=[pltpu.VMEM((N_HEADS, HEAD_DIM, HEAD_DIM), F32),
                        pltpu.VMEM((SUBLANES, HEAD_DIM), F32),
                        pltpu.VMEM((SUBLANES, LANES), F32),
                        pltpu.VMEM((CONV_PAD + t_tile, d), F32),
                        pltpu.VMEM((2, SUBLANES, CONV_PAD + t_tile, LANES), F32)],
        compiler_params=pltpu.CompilerParams(
            dimension_semantics=("arbitrary", "arbitrary"), vmem_limit_bytes=VMEM_LIMIT_SMALL),
        name="seqmix_prompt",
    )(q, k, v, og, g_t, glu, mh, w_dw, b_dw)


def _seqmix_sample_kernel(q_ref, k_ref, v_ref, og_ref, gr_ref, c_in, n_in, m_in, glu_ref, cst_ref,
                          mh_ref, wdw_ref, bdw_ref,
                          hb_ref, ycv_ref, c_out, n_out, m_out, cso_ref, cbuf):
    sb, t, d = q_ref.shape
    cbuf[:, CONV_OFF:CONV_PAD, :] = cst_ref[...]
    cbuf[:, CONV_PAD:CONV_PAD + t, :] = glu_ref[...]
    for lb in range(d // LANES):
        ls = slice(lb * LANES, (lb + 1) * LANES)
        w_l = wdw_ref[:, ls]
        acc = jnp.broadcast_to(bdw_ref[:, ls], (sb, t, LANES))
        for j in range(CONV_WIDTH):
            acc = acc + w_l[j:j + 1, :] * cbuf[:, CONV_OFF + j:CONV_OFF + j + t, ls]
        ycv_ref[:, :, ls] = acc
    cso_ref[...] = cbuf[:, CONV_OFF + t:CONV_PAD + t, :]

    lane = lax.broadcasted_iota(jnp.int32, (SUBLANES, LANES), 1)
    for s in range(sb):
        g8 = gr_ref[s]
        lf8 = _log_sigmoid(g8)
        bc8 = _seg_cumsum_lanes(lf8, t)
        lf = lf8[N_HEADS:2 * N_HEADS, :]
        gg = g8[0:N_HEADS, :] - bc8[N_HEADS:2 * N_HEADS, :]
        m_blk = m_in[s]
        m_acc = jnp.zeros((SUBLANES, LANES), F32)
        for hd in range(N_HEADS):
            cols = slice(hd * HEAD_DIM, (hd + 1) * HEAD_DIM)
            h, c_new, n_new, m_new = _mlstm_head_chunk(
                q_ref[s, :, cols], k_ref[s, :, cols], v_ref[s, :, cols],
                gg[hd:hd + 1, 0:t], lf[hd:hd + 1, 0:t],
                c_in[s, hd], n_in[s, hd:hd + 1, :], m_blk[0:1, hd:hd + 1])
            c_out[s, hd] = c_new
            n_out[s, hd:hd + 1, :] = n_new
            m_acc = jnp.where(lane == hd, m_new, m_acc)
            hb_ref[s, :, cols] = _head_out(h, og_ref[s, :, cols], mh_ref[:, cols]).astype(hb_ref.dtype)
        m_out[s] = m_acc


def _seqmix_sample_call(q, k, v, og, g_r, c0, n0, m0, glu, conv_state, mh, w_dw, b_dw, *, seq_block):
    b, t, d = q.shape
    tok = pl.BlockSpec((seq_block, t, d), lambda i: (i, 0, 0))
    small = pl.BlockSpec((seq_block, SUBLANES, LANES), lambda i: (i, 0, 0))
    c_spec = pl.BlockSpec((seq_block, N_HEADS, HEAD_DIM, HEAD_DIM), lambda i: (i, 0, 0, 0))
    n_spec = pl.BlockSpec((seq_block, N_HEADS, HEAD_DIM), lambda i: (i, 0, 0))
    st_spec = pl.BlockSpec((None, seq_block, CONV_STATE, d), lambda i: (0, i, 0, 0))
    return pl.pallas_call(
        _seqmix_sample_kernel,
        grid=(b // seq_block,),
        in_specs=[tok, tok, tok, tok, small, c_spec, n_spec, small, tok, st_spec,
                  _const_spec((1, d)), _const_spec(w_dw.shape), _const_spec((1, d))],
        out_specs=[tok, tok, c_spec, n_spec, small, st_spec],
        out_shape=[jax.ShapeDtypeStruct((b, t, d), F32),
                   jax.ShapeDtypeStruct((b, t, d), F32),
                   jax.ShapeDtypeStruct(c0.shape, F32),
                   jax.ShapeDtypeStruct(n0.shape, F32),
                   jax.ShapeDtypeStruct((b, SUBLANES, LANES), F32),
                   jax.ShapeDtypeStruct((1, b, CONV_STATE, d), F32)],
        scratch_shapes=[pltpu.VMEM((seq_block, CONV_PAD + t, d), F32)],
        compiler_params=pltpu.CompilerParams(
            dimension_semantics=("arbitrary",), vmem_limit_bytes=VMEM_LIMIT_BIG),
        name="seqmix_sample",
    )(q, k, v, og, g_r, c0, n0, m0, glu, conv_state, mh, w_dw, b_dw)


def _mix_tail(x1, yc3, hb, ga, gb, mod, lng_ref, lnb_ref, wco_ref, wmo_ref, wo_ref,
              n3_ref, wg_ref, wu_ref, wd_ref, fn_ref):
    ns, t, d = x1.shape
    rows = ns * t
    yc = yc3.reshape(rows, d)
    mu = jnp.mean(yc, axis=-1, keepdims=True)
    yd = yc - mu
    var = jnp.mean(yd * yd, axis=-1, keepdims=True)
    ln = yd * lax.rsqrt(var + EPS) * lng_ref[...] + lnb_ref[...]
    a_in = (ln * _sigmoid(ln)).astype(BF16)
    a_out = jnp.dot(a_in, wco_ref[...], preferred_element_type=F32)
    b_out = jnp.dot(hb.reshape(rows, d).astype(BF16), wmo_ref[...], preferred_element_type=F32)
    z = (ga.reshape(rows, d).astype(F32) * a_out + gb.reshape(rows, d).astype(F32) * b_out).astype(BF16)
    zo = jnp.dot(z, wo_ref[...], preferred_element_type=F32)
    g2, sh3, sc3, g3 = [mod[:, i:i + 1, :] for i in range(4)]
    x2 = x1 + g2 * zo.reshape(ns, t, d)
    x3 = _ffn_block(x2, n3_ref[...], sh3, sc3, g3, wg_ref, wu_ref, wd_ref)
    return _rms(x3, fn_ref[...])


def _tail_kernel(x1_ref, ycv_ref, hb_ref, ga_ref, gb_ref, mod_ref, *rest):
    *tail_w, y_ref = rest
    y_ref[...] = _mix_tail(x1_ref[...], ycv_ref[...], hb_ref[...], ga_ref[...], gb_ref[...], mod_ref[...], *tail_w)


def _tail_call(x1, ycv, hb, ga, gb, mod_b, weights, *, nseq, t_tile):
    b, s, d = x1.shape
    tok = pl.BlockSpec((nseq, t_tile, d), lambda i, j: (i, j, 0))
    in_specs = ([tok] * 5 + [pl.BlockSpec((nseq, 4, d), lambda i, j: (i, 0, 0))]
                + [_const_spec(w.shape) for w in weights])
    return pl.pallas_call(
        _tail_kernel,
        grid=(b // nseq, s // t_tile),
        in_specs=in_specs,
        out_specs=tok,
        out_shape=jax.ShapeDtypeStruct((b, s, d), F32),
        compiler_params=pltpu.CompilerParams(
            dimension_semantics=("arbitrary", "arbitrary"), vmem_limit_bytes=VMEM_LIMIT_BIG),
        name="tail",
    )(x1, ycv, hb, ga, gb, mod_b, *weights)


def _layer(xp, xs, c_all, st_conv, st_c, st_n, st_m, final_norm,
           w_ada, b_ada, norm_ffn1, w1_gate, w1_up, w1_down, norm_mix, w_in, b_in, w_dw, b_dw,
           ln_conv_g, ln_conv_b, w_conv_out, mh_norm, w_mlstm_out, w_out, norm_ffn2, w2_gate, w2_up, w2_down):
    bp = xp.shape[0]
    bs, ts, d = xs.shape
    row = lambda a: a.reshape(1, -1)

    mod = _mod_call(c_all, w_ada, b_ada).reshape(-1, N_MOD, d)
    mod_a, mod_b = mod[:, 0:5, :], mod[:, 5:N_MOD, :]

    n_in = w_in.shape[1]
    assert n_in == 8 * d + 2 * N_HEADS
    col_pad = -n_in % LANES
    w_main = jnp.pad(w_in, ((0, 0), (0, col_pad))).astype(BF16)
    b_main = row(jnp.pad(b_in, (0, col_pad)))
    ffn1 = (row(norm_ffn1), w1_gate.astype(BF16), w1_up.astype(BF16), w1_down.astype(BF16))
    in_w = (row(norm_mix), w_main, b_main)
    conv_w = (row(mh_norm), jnp.concatenate([w_dw, jnp.zeros((CONV_PAD - CONV_WIDTH, d), F32)], axis=0), row(b_dw))
    tail_w = (row(ln_conv_g), row(ln_conv_b),
              w_conv_out.astype(BF16), w_mlstm_out.astype(BF16), w_out.astype(BF16),
              row(norm_ffn2), w2_gate.astype(BF16), w2_up.astype(BF16), w2_down.astype(BF16), row(final_norm))

    x1, glu, q, k, v, og, ga, gb, g_t = _ffn_in_call(
        xp, mod_a[:bp], *ffn1, *in_w, nseq=1, t_tile=256, act_dtype=BF16)
    hb, ycv, c_p, n_p, m_p, conv_p = _seqmix_prompt_call(q, k, v, og, g_t, glu, *conv_w, t_tile=512)
    y_p = _tail_call(x1, ycv, hb, ga, gb, mod_b[:bp], tail_w, nseq=1, t_tile=512)

    x1, glu, q, k, v, og, ga, gb, g_t = _ffn_in_call(
        xs, mod_a[bp:], *ffn1, *in_w, nseq=32, t_tile=ts, act_dtype=F32)
    g_r = g_t.reshape(2 * N_HEADS, bs, ts).transpose(1, 0, 2)
    g_r = jnp.pad(g_r, ((0, 0), (0, 0), (0, LANES - ts)))
    m_b = jnp.broadcast_to(jnp.pad(st_m, ((0, 0), (0, LANES - N_HEADS)))[:, None, :], (bs, SUBLANES, LANES))
    hb, ycv, c_s, n_s, m_s, conv_s = _seqmix_sample_call(
        q, k, v, og, g_r, st_c, st_n, m_b, glu, st_conv, *conv_w, seq_block=8)
    y_s = _tail_call(x1, ycv, hb, ga, gb, mod_b[bp:], tail_w, nseq=32, t_tile=ts)

    return (y_p, y_s, conv_p, c_p, n_p, m_p[:, :N_HEADS, 0], conv_s, c_s, n_s, m_s[:, 0, :N_HEADS])


def kernel(x_prompt, x_sample, c_prompt, c_sample, state_conv, state_C, state_n, state_m, w_ada, b_ada, norm_ffn1, w1_gate, w1_up, w1_down, norm_mix, w_in, b_in, w_dw, b_dw, ln_conv_g, ln_conv_b, w_conv_out, mh_norm, w_mlstm_out, w_out, norm_ffn2, w2_gate, w2_up, w2_down, final_norm):
    depth = w_ada.shape[0]
    assert depth == 1, "the fused final norm assumes a single layer"
    c_all = jnp.concatenate([c_prompt, c_sample], axis=0)
    layer_w = (w_ada, b_ada, norm_ffn1, w1_gate, w1_up, w1_down, norm_mix, w_in, b_in, w_dw, b_dw,
               ln_conv_g, ln_conv_b, w_conv_out, mh_norm, w_mlstm_out, w_out, norm_ffn2, w2_gate, w2_up, w2_down)
    outs = _layer(x_prompt, x_sample, c_all, state_conv, state_C[0], state_n[0], state_m[0], final_norm,
                  *[w[0] for w in layer_w])
    conv_idx = (2, 6)
    return tuple(o if i < 2 or i in conv_idx else o[None] for i, o in enumerate(outs))
```

```python
import jax
import jax.numpy as jnp
from jax import lax
from jax.experimental import pallas as pl
from jax.experimental.pallas import tpu as pltpu

F32 = jnp.float32
BF16 = jnp.bfloat16

D_MODEL = 1024
N_HEADS = 4
HEAD_DIM = 256
CONV_WIDTH = 31
CONV_STATE = CONV_WIDTH - 1
D_FF = 2816
N_MOD = 9
CHUNK = 128
EPS = 1e-6

LANES = 128
SUBLANES = 8
CONV_PAD = 32
CONV_OFF = CONV_PAD - CONV_STATE
VMEM_LIMIT_BIG = 58 * 1024 * 1024
VMEM_LIMIT_SMALL = 40 * 1024 * 1024


def _sigmoid(x):
    return 1.0 / (1.0 + jnp.exp(-x))


def _log_sigmoid(x):
    return jnp.minimum(x, 0.0) - jnp.log1p(jnp.exp(-jnp.abs(x)))


def _rms(x, g):
    return x * lax.rsqrt(jnp.mean(x * x, axis=-1, keepdims=True) + EPS) * g


def _const_spec(shape):
    zeros = (0,) * len(shape)
    return pl.BlockSpec(shape, lambda *_: zeros, pipeline_mode=pl.Buffered(1))


def _mod_kernel(c_ref, w_ref, b_ref, o_ref):
    c = c_ref[...]
    a = (c * _sigmoid(c)).astype(BF16)
    o_ref[...] = jnp.dot(a, w_ref[...].astype(BF16), preferred_element_type=F32) + b_ref[...]


def _mod_call(c, w_ada, b_ada):
    rows = c.shape[0]
    n_out = w_ada.shape[1]
    bn = n_out // 8
    return pl.pallas_call(
        _mod_kernel,
        grid=(n_out // bn,),
        in_specs=[
            pl.BlockSpec((rows, D_MODEL), lambda j: (0, 0)),
            pl.BlockSpec((D_MODEL, bn), lambda j: (0, j)),
            pl.BlockSpec((1, bn), lambda j: (0, j)),
        ],
        out_specs=pl.BlockSpec((rows, bn), lambda j: (0, j)),
        out_shape=jax.ShapeDtypeStruct((rows, n_out), F32),
        compiler_params=pltpu.CompilerParams(
            dimension_semantics=("arbitrary",), vmem_limit_bytes=VMEM_LIMIT_SMALL),
        name="mod",
    )(c, w_ada, b_ada.reshape(1, n_out))


def _ffn_block(x3, nrm, sh, sc, gt, wg_ref, wu_ref, wd_ref):
    ns, t, d = x3.shape
    h = _rms(x3, nrm) * (1.0 + sc) + sh
    hb = h.reshape(ns * t, d).astype(BF16)
    g = jnp.dot(hb, wg_ref[...], preferred_element_type=F32)
    u = jnp.dot(hb, wu_ref[...], preferred_element_type=F32)
    a = (g * _sigmoid(g) * u).astype(BF16)
    dn = jnp.dot(a, wd_ref[...], preferred_element_type=F32)
    return x3 + (0.5 * gt) * dn.reshape(ns, t, d)


def _ffn_in_kernel(x_ref, mod_ref, n1_ref, wg_ref, wu_ref, wd_ref, n2_ref, wm_ref, bm_ref,
                   x1_ref, glu_ref, q_ref, k_ref, v_ref, og_ref, ga_ref, gb_ref, gt_ref):
    x = x_ref[...]
    ns, t, d = x.shape
    mod = mod_ref[...]
    sh1, sc1, g1, sh2, sc2 = [mod[:, i:i + 1, :] for i in range(5)]
    x1 = _ffn_block(x, n1_ref[...], sh1, sc1, g1, wg_ref, wu_ref, wd_ref)
    x1_ref[...] = x1
    u = _rms(x1, n2_ref[...]) * (1.0 + sc2) + sh2
    ub = u.reshape(ns * t, d).astype(BF16)

    def proj(j):
        cols = slice(j * d, (j + 1) * d)
        return jnp.dot(ub, wm_ref[:, cols], preferred_element_type=F32) + bm_ref[:, cols]

    def put(ref, val):
        ref[...] = val.reshape(ns, t, d).astype(ref.dtype)

    put(glu_ref, proj(0) * _sigmoid(proj(1)))
    put(q_ref, proj(2) * (HEAD_DIM ** -0.5))
    put(k_ref, proj(3))
    put(v_ref, proj(4))
    put(og_ref, _sigmoid(proj(5)))
    put(ga_ref, _sigmoid(proj(6)))
    put(gb_ref, _sigmoid(proj(7)))
    n_gates = gt_ref.shape[0]
    gates = jnp.dot(ub, wm_ref[:, 8 * d:], preferred_element_type=F32) + bm_ref[:, 8 * d:]
    gt_ref[...] = gates.T[0:n_gates, :]


def _ffn_in_call(x, mod_a, nrm1, wg, wu, wd, nrm2, w_main, b_main, *, nseq, t_tile, act_dtype):
    b, s, d = x.shape
    grid = (b // nseq, s // t_tile)
    rows = nseq * t_tile
    tok = pl.BlockSpec((nseq, t_tile, d), lambda i, j: (i, j, 0))
    n_t = s // t_tile
    in_specs = [
        tok,
        pl.BlockSpec((nseq, 5, d), lambda i, j: (i, 0, 0)),
        _const_spec((1, d)),
        _const_spec(wg.shape), _const_spec(wu.shape), _const_spec(wd.shape),
        _const_spec((1, d)),
        _const_spec(w_main.shape), _const_spec(b_main.shape),
    ]
    out_specs = [tok] * 8 + [pl.BlockSpec((2 * N_HEADS, rows), lambda i, j: (0, i * n_t + j))]
    out_shape = ([jax.ShapeDtypeStruct((b, s, d), F32)] * 2
                 + [jax.ShapeDtypeStruct((b, s, d), act_dtype)] * 6
                 + [jax.ShapeDtypeStruct((2 * N_HEADS, b * s), F32)])
    return pl.pallas_call(
        _ffn_in_kernel,
        grid=grid,
        in_specs=in_specs,
        out_specs=out_specs,
        out_shape=out_shape,
        compiler_params=pltpu.CompilerParams(
            dimension_semantics=("arbitrary", "arbitrary"), vmem_limit_bytes=VMEM_LIMIT_BIG),
        name="ffn_in",
    )(x, mod_a, nrm1, wg, wu, wd, nrm2, w_main, b_main)


def _seg_cumsum_lanes(x, seg):
    lane = lax.broadcasted_iota(jnp.int32, x.shape, 1)
    pos = lane & (LANES - 1)
    sh = 1
    while sh < seg:
        x = x + jnp.where(pos >= sh, pltpu.roll(x, sh, 1), 0.0)
        sh *= 2
    return x


def _mlstm_head_chunk(q, k, v, g_row, lf_row, c_st, n_row, m):
    l = q.shape[0]
    t_idx = lax.broadcasted_iota(jnp.int32, (l, l), 0)
    s_idx = lax.broadcasted_iota(jnp.int32, (l, l), 1)
    causal = s_idx <= t_idx
    gb = jnp.broadcast_to(g_row, (l, l))
    lfb = jnp.broadcast_to(lf_row, (l, l))
    cm_col = jnp.max(jnp.where(causal, gb, -jnp.inf), axis=1, keepdims=True)
    bcum_col = jnp.sum(jnp.where(causal, lfb, 0.0), axis=1, keepdims=True)
    g_col = jnp.sum(jnp.where(s_idx == t_idx, gb, 0.0), axis=1, keepdims=True)
    g_max = cm_col[l - 1:l, :]
    qb, kb, vb = q.astype(BF16), k.astype(BF16), v.astype(BF16)

    s = lax.dot_general(qb, kb, (((1,), (1,)), ((), ())), preferred_element_type=F32)
    p = (s * jnp.where(causal, jnp.exp(gb - cm_col), 0.0)).astype(BF16)
    pv = jnp.dot(p, vb, preferred_element_type=F32)
    den_loc = jnp.dot(p, jnp.ones((l, LANES), BF16), preferred_element_type=F32)
    ws_loc = jnp.exp(g_col - g_max)
    vs = (v.astype(F32) * ws_loc).astype(BF16)
    upd = lax.dot_general(vs, kb, (((0,), (0,)), ((), ())), preferred_element_type=F32)
    ws_rows = jnp.broadcast_to(jnp.exp(g_row - g_max), (SUBLANES, l)).astype(BF16)
    n_upd = jnp.dot(ws_rows, kb, preferred_element_type=F32)[0:1, :]

    m_col = jnp.maximum(m, cm_col)
    r_col = jnp.exp(cm_col - m_col)
    w_inter = jnp.exp(m - m_col)
    qc = lax.dot_general(qb, c_st.astype(BF16), (((1,), (1,)), ((), ())), preferred_element_type=F32)
    n_rep = jnp.broadcast_to(n_row, (LANES, n_row.shape[1])).astype(BF16)
    qn = lax.dot_general(qb, n_rep, (((1,), (1,)), ((), ())), preferred_element_type=F32)
    num = r_col * pv + w_inter * qc
    den = jnp.maximum(jnp.abs(r_col * den_loc + w_inter * qn), jnp.exp(-(bcum_col + m_col)))
    inv_den = 1.0 / den
    h = num * jnp.concatenate([inv_den] * (num.shape[1] // LANES), axis=1)
    m_last = m_col[l - 1:l, :]
    a_prev = jnp.exp(m - m_last)
    scale = jnp.exp(g_max - m_last)
    c_new = a_prev * c_st + scale * upd
    n_new = a_prev * n_row + scale * n_upd
    m_new = bcum_col[l - 1:l, :] + m_last
    return h, c_new, n_new, m_new


def _head_out(h, og, mh):
    hn = h * lax.rsqrt(jnp.mean(h * h, axis=-1, keepdims=True) + EPS) * mh
    if og.dtype == BF16:
        return og * hn.astype(BF16)
    return og * hn


CONV_ROW_BLOCK = 8 * SUBLANES


def _conv_prompt_tile(cbuf, xs, wdw_ref, bdw_ref, ycv_ref):
    n_rows, d = cbuf.shape
    t = n_rows - CONV_PAD
    span = n_rows - SUBLANES
    for lb in range(d // LANES):
        ls = slice(lb * LANES, (lb + 1) * LANES)
        xl = xs.at[lb % 2]
        for r in range(1, SUBLANES):
            xl[r, 0:span, :] = cbuf[r:r + span, ls]
        w_l = wdw_ref[:, ls]
        b_l = bdw_ref[:, ls]
        for r0 in range(0, t, CONV_ROW_BLOCK):
            acc = jnp.broadcast_to(b_l, (CONV_ROW_BLOCK, LANES))
            for j in range(CONV_WIDTH):
                r = (CONV_OFF + j) % SUBLANES
                lo = CONV_OFF + j - r + r0
                src = cbuf[lo:lo + CONV_ROW_BLOCK, ls] if r == 0 else xl[r, lo:lo + CONV_ROW_BLOCK, :]
                acc = acc + w_l[j:j + 1, :] * src
            ycv_ref[0, r0:r0 + CONV_ROW_BLOCK, ls] = acc


def _seqmix_prompt_kernel(q_ref, k_ref, v_ref, og_ref, gt_ref, glu_ref, mh_ref, wdw_ref, bdw_ref,
                          hb_ref, ycv_ref, c_out, n_out, m_out, cso_ref, c_s, n_s, m_s, cbuf, xs):
    t = pl.program_id(1)
    tm = q_ref.shape[1]

    @pl.when(t == 0)
    def _():
        c_s[...] = jnp.zeros_like(c_s)
        n_s[...] = jnp.zeros_like(n_s)
        m_s[...] = jnp.zeros_like(m_s)
        cbuf[0:CONV_PAD, :] = jnp.zeros((CONV_PAD, cbuf.shape[1]), F32)

    @pl.when(t > 0)
    def _():
        cbuf[0:CONV_PAD, :] = cbuf[tm:tm + CONV_PAD, :]

    cbuf[CONV_PAD:CONV_PAD + tm, :] = glu_ref[0]
    _conv_prompt_tile(cbuf, xs, wdw_ref, bdw_ref, ycv_ref)
    cso_ref[0] = cbuf[CONV_OFF + tm:CONV_PAD + tm, :]

    g8 = gt_ref[...]
    lf8 = _log_sigmoid(g8)
    bc8 = _seg_cumsum_lanes(lf8, CHUNK)
    lf = lf8[N_HEADS:2 * N_HEADS, :]
    gg = g8[0:N_HEADS, :] - bc8[N_HEADS:2 * N_HEADS, :]
    for c in range(tm // CHUNK):
        rows = slice(c * CHUNK, (c + 1) * CHUNK)
        for hd in range(N_HEADS):
            cols = slice(hd * HEAD_DIM, (hd + 1) * HEAD_DIM)
            h, c_new, n_new, m_new = _mlstm_head_chunk(
                q_ref[0, rows, cols], k_ref[0, rows, cols], v_ref[0, rows, cols],
                gg[hd:hd + 1, rows], lf[hd:hd + 1, rows],
                c_s[hd], n_s[hd:hd + 1, :], m_s[hd:hd + 1, 0:1])
            c_s[hd] = c_new
            n_s[hd:hd + 1, :] = n_new
            m_s[hd:hd + 1, :] = jnp.broadcast_to(m_new, (1, LANES))
            hb_ref[0, rows, cols] = _head_out(h, og_ref[0, rows, cols], mh_ref[:, cols]).astype(hb_ref.dtype)

    @pl.when(t == pl.num_programs(1) - 1)
    def _():
        c_out[0] = c_s[...]
        n_out[0] = n_s[0:N_HEADS, :]
        m_out[0] = m_s[...]


def _seqmix_prompt_call(q, k, v, og, g_t, glu, mh, w_dw, b_dw, *, t_tile):
    b, s, d = q.shape
    n_t = s // t_tile
    tok = pl.BlockSpec((1, t_tile, d), lambda i, j: (i, j, 0))
    return pl.pallas_call(
        _seqmix_prompt_kernel,
        grid=(b, n_t),
        in_specs=[tok, tok, tok, tok,
                  pl.BlockSpec((2 * N_HEADS, t_tile), lambda i, j: (0, i * n_t + j)),
                  tok, _const_spec((1, d)), _const_spec(w_dw.shape), _const_spec((1, d))],
        out_specs=[tok, tok,
                   pl.BlockSpec((1, N_HEADS, HEAD_DIM, HEAD_DIM), lambda i, j: (i, 0, 0, 0)),
                   pl.BlockSpec((1, N_HEADS, HEAD_DIM), lambda i, j: (i, 0, 0)),
                   pl.BlockSpec((1, SUBLANES, LANES), lambda i, j: (i, 0, 0)),
                   pl.BlockSpec((None, 1, CONV_STATE, d), lambda i, j: (0, i, 0, 0))],
        out_shape=[jax.ShapeDtypeStruct((b, s, d), BF16),
                   jax.ShapeDtypeStruct((b, s, d), F32),
                   jax.ShapeDtypeStruct((b, N_HEADS, HEAD_DIM, HEAD_DIM), F32),
                   jax.ShapeDtypeStruct((b, N_HEADS, HEAD_DIM), F32),
                   jax.ShapeDtypeStruct((b, SUBLANES, LANES), F32),
                   jax.ShapeDtypeStruct((1, b, CONV_STATE, d), F32)],
        scratch_shapes=[pltpu.VMEM((N_HEADS, HEAD_DIM, HEAD_DIM), F32),
                        pltpu.VMEM((SUBLANES, HEAD_DIM), F32),
                        pltpu.VMEM((SUBLANES, LANES), F32),
                        pltpu.VMEM((CONV_PAD + t_tile, d), F32),
                        pltpu.VMEM((2, SUBLANES, CONV_PAD + t_tile, LANES), F32)],
        compiler_params=pltpu.CompilerParams(
            dimension_semantics=("arbitrary", "arbitrary"), vmem_limit_bytes=VMEM_LIMIT_SMALL),
        name="seqmix_prompt",
    )(q, k, v, og, g_t, glu, mh, w_dw, b_dw)


def _seqmix_sample_kernel(q_ref, k_ref, v_ref, og_ref, gr_ref, c_in, n_in, m_in, glu_ref, cst_ref,
                          mh_ref, wdw_ref, bdw_ref,
                          hb_ref, ycv_ref, c_out, n_out, m_out, cso_ref, cbuf):
    sb, t, d = q_ref.shape
    cbuf[:, CONV_OFF:CONV_PAD, :] = cst_ref[...]
    cbuf[:, CONV_PAD:CONV_PAD + t, :] = glu_ref[...]
    assert t == SUBLANES
    row_id = lax.broadcasted_iota(jnp.int32, (SUBLANES, LANES), 0)
    for lb in range(d // LANES):
        ls = slice(lb * LANES, (lb + 1) * LANES)
        w_l = wdw_ref[:, ls]
        b_l = jnp.broadcast_to(bdw_ref[:, ls], (SUBLANES, LANES))
        for s in range(sb):
            groups = [cbuf[s, g * SUBLANES:(g + 1) * SUBLANES, ls] for g in range((CONV_PAD + t) // SUBLANES)]
            acc = b_l
            for j in range(CONV_WIDTH):
                g, r = divmod(CONV_OFF + j, SUBLANES)
                if r == 0:
                    win = groups[g]
                else:
                    win = pltpu.roll(jnp.where(row_id >= r, groups[g], groups[g + 1]), SUBLANES - r, 0)
                acc = acc + w_l[j:j + 1, :] * win
            ycv_ref[s, :, ls] = acc
    cso_ref[...] = cbuf[:, CONV_OFF + t:CONV_PAD + t, :]

    lane = lax.broadcasted_iota(jnp.int32, (SUBLANES, LANES), 1)

    for s in range(sb):
        g8 = gr_ref[s]
        lf8 = _log_sigmoid(g8)
        bc8 = _seg_cumsum_lanes(lf8, t)
        lf = lf8[N_HEADS:2 * N_HEADS, :]
        gg = g8[0:N_HEADS, :] - bc8[N_HEADS:2 * N_HEADS, :]
        m_blk = m_in[s]
        m_acc = jnp.zeros((SUBLANES, LANES), F32)
        for hd in range(N_HEADS):
            cols = slice(hd * HEAD_DIM, (hd + 1) * HEAD_DIM)
            h, c_new, n_new, m_new = _mlstm_head_chunk(
                q_ref[s, :, cols], k_ref[s, :, cols], v_ref[s, :, cols],
                gg[hd:hd + 1, 0:t], lf[hd:hd + 1, 0:t],
                c_in[s, hd], n_in[s, hd:hd + 1, :], m_blk[0:1, hd:hd + 1])
            c_out[s, hd] = c_new
            n_out[s, hd:hd + 1, :] = n_new
            m_acc = jnp.where(lane == hd, m_new, m_acc)
            hb_ref[s, :, cols] = _head_out(h, og_ref[s, :, cols], mh_ref[:, cols]).astype(hb_ref.dtype)
        m_out[s] = m_acc


def _seqmix_sample_call(q, k, v, og, g_r, c0, n0, m0, glu, conv_state, mh, w_dw, b_dw, *, seq_block):
    b, t, d = q.shape
    tok = pl.BlockSpec((seq_block, t, d), lambda i: (i, 0, 0))
    small = pl.BlockSpec((seq_block, SUBLANES, LANES), lambda i: (i, 0, 0))
    c_spec = pl.BlockSpec((seq_block, N_HEADS, HEAD_DIM, HEAD_DIM), lambda i: (i, 0, 0, 0))
    n_spec = pl.BlockSpec((seq_block, N_HEADS, HEAD_DIM), lambda i: (i, 0, 0))
    st_spec = pl.BlockSpec((None, seq_block, CONV_STATE, d), lambda i: (0, i, 0, 0))
    return pl.pallas_call(
        _seqmix_sample_kernel,
        grid=(b // seq_block,),
        in_specs=[tok, tok, tok, tok, small, c_spec, n_spec, small, tok, st_spec,
                  _const_spec((1, d)), _const_spec(w_dw.shape), _const_spec((1, d))],
        out_specs=[tok, tok, c_spec, n_spec, small, st_spec],
        out_shape=[jax.ShapeDtypeStruct((b, t, d), F32),
                   jax.ShapeDtypeStruct((b, t, d), F32),
                   jax.ShapeDtypeStruct(c0.shape, F32),
                   jax.ShapeDtypeStruct(n0.shape, F32),
                   jax.ShapeDtypeStruct((b, SUBLANES, LANES), F32),
                   jax.ShapeDtypeStruct((1, b, CONV_STATE, d), F32)],
        scratch_shapes=[pltpu.VMEM((seq_block, CONV_PAD + t, d), F32)],
        compiler_params=pltpu.CompilerParams(
            dimension_semantics=("arbitrary",), vmem_limit_bytes=VMEM_LIMIT_BIG),
        name="seqmix_sample",
    )(q, k, v, og, g_r, c0, n0, m0, glu, conv_state, mh, w_dw, b_dw)


def _mix_tail(x1, yc3, hb, ga, gb, mod, lng_ref, lnb_ref, wco_ref, wmo_ref, wo_ref,
              n3_ref, wg_ref, wu_ref, wd_ref, fn_ref):
    ns, t, d = x1.shape
    rows = ns * t
    yc = yc3.reshape(rows, d)
    mu = jnp.mean(yc, axis=-1, keepdims=True)
    yd = yc - mu
    var = jnp.mean(yd * yd, axis=-1, keepdims=True)
    ln = yd * lax.rsqrt(var + EPS) * lng_ref[...] + lnb_ref[...]
    a_in = (ln * _sigmoid(ln)).astype(BF16)
    a_out = jnp.dot(a_in, wco_ref[...], preferred_element_type=F32)
    b_out = jnp.dot(hb.reshape(rows, d).astype(BF16), wmo_ref[...], preferred_element_type=F32)
    z = (ga.reshape(rows, d).astype(F32) * a_out + gb.reshape(rows, d).astype(F32) * b_out).astype(BF16)
    zo = jnp.dot(z, wo_ref[...], preferred_element_type=F32)
    g2, sh3, sc3, g3 = [mod[:, i:i + 1, :] for i in range(4)]
    x2 = x1 + g2 * zo.reshape(ns, t, d)
    x3 = _ffn_block(x2, n3_ref[...], sh3, sc3, g3, wg_ref, wu_ref, wd_ref)
    return _rms(x3, fn_ref[...])


def _tail_kernel(x1_ref, ycv_ref, hb_ref, ga_ref, gb_ref, mod_ref, *rest):
    *tail_w, y_ref = rest
    y_ref[...] = _mix_tail(x1_ref[...], ycv_ref[...], hb_ref[...], ga_ref[...], gb_ref[...], mod_ref[...], *tail_w)


def _tail_call(x1, ycv, hb, ga, gb, mod_b, weights, *, nseq, t_tile):
    b, s, d = x1.shape
    tok = pl.BlockSpec((nseq, t_tile, d), lambda i, j: (i, j, 0))
    in_specs = ([tok] * 5 + [pl.BlockSpec((nseq, 4, d), lambda i, j: (i, 0, 0))]
                + [_const_spec(w.shape) for w in weights])
    return pl.pallas_call(
        _tail_kernel,
        grid=(b // nseq, s // t_tile),
        in_specs=in_specs,
        out_specs=tok,
        out_shape=jax.ShapeDtypeStruct((b, s, d), F32),
        compiler_params=pltpu.CompilerParams(
            dimension_semantics=("arbitrary", "arbitrary"), vmem_limit_bytes=VMEM_LIMIT_BIG),
        name="tail",
    )(x1, ycv, hb, ga, gb, mod_b, *weights)


def _layer(xp, xs, c_all, st_conv, st_c, st_n, st_m, final_norm,
           w_ada, b_ada, norm_ffn1, w1_gate, w1_up, w1_down, norm_mix, w_in, b_in, w_dw, b_dw,
           ln_conv_g, ln_conv_b, w_conv_out, mh_norm, w_mlstm_out, w_out, norm_ffn2, w2_gate, w2_up, w2_down):
    bp = xp.shape[0]
    bs, ts, d = xs.shape
    row = lambda a: a.reshape(1, -1)

    mod = _mod_call(c_all, w_ada, b_ada).reshape(-1, N_MOD, d)
    mod_a, mod_b = mod[:, 0:5, :], mod[:, 5:N_MOD, :]

    n_in = w_in.shape[1]
    assert n_in == 8 * d + 2 * N_HEADS
    col_pad = -n_in % LANES
    w_main = jnp.pad(w_in, ((0, 0), (0, col_pad))).astype(BF16)
    b_main = row(jnp.pad(b_in, (0, col_pad)))
    ffn1 = (row(norm_ffn1), w1_gate.astype(BF16), w1_up.astype(BF16), w1_down.astype(BF16))
    in_w = (row(norm_mix), w_main, b_main)
    conv_w = (row(mh_norm), jnp.concatenate([w_dw, jnp.zeros((CONV_PAD - CONV_WIDTH, d), F32)], axis=0), row(b_dw))
    tail_w = (row(ln_conv_g), row(ln_conv_b),
              w_conv_out.astype(BF16), w_mlstm_out.astype(BF16), w_out.astype(BF16),
              row(norm_ffn2), w2_gate.astype(BF16), w2_up.astype(BF16), w2_down.astype(BF16), row(final_norm))

    x1, glu, q, k, v, og, ga, gb, g_t = _ffn_in_call(
        xp, mod_a[:bp], *ffn1, *in_w, nseq=1, t_tile=256, act_dtype=BF16)
    hb, ycv, c_p, n_p, m_p, conv_p = _seqmix_prompt_call(q, k, v, og, g_t, glu, *conv_w, t_tile=512)
    y_p = _tail_call(x1, ycv, hb, ga, gb, mod_b[:bp], tail_w, nseq=1, t_tile=512)

    x1, glu, q, k, v, og, ga, gb, g_t = _ffn_in_call(
        xs, mod_a[bp:], *ffn1, *in_w, nseq=32, t_tile=ts, act_dtype=F32)
    g_r = g_t.reshape(2 * N_HEADS, bs, ts).transpose(1, 0, 2)
    g_r = jnp.pad(g_r, ((0, 0), (0, 0), (0, LANES - ts)))
    m_b = jnp.broadcast_to(jnp.pad(st_m, ((0, 0), (0, LANES - N_HEADS)))[:, None, :], (bs, SUBLANES, LANES))
    hb, ycv, c_s, n_s, m_s, conv_s = _seqmix_sample_call(
        q, k, v, og, g_r, st_c, st_n, m_b, glu, st_conv, *conv_w, seq_block=8)
    y_s = _tail_call(x1, ycv, hb, ga, gb, mod_b[bp:], tail_w, nseq=32, t_tile=ts)

    return (y_p, y_s, conv_p, c_p, n_p, m_p[:, :N_HEADS, 0], conv_s, c_s, n_s, m_s[:, 0, :N_HEADS])


def kernel(x_prompt, x_sample, c_prompt, c_sample, state_conv, state_C, state_n, state_m, w_ada, b_ada, norm_ffn1, w1_gate, w1_up, w1_down, norm_mix, w_in, b_in, w_dw, b_dw, ln_conv_g, ln_conv_b, w_conv_out, mh_norm, w_mlstm_out, w_out, norm_ffn2, w2_gate, w2_up, w2_down, final_norm):
    depth = w_ada.shape[0]
    assert depth == 1, "the fused final norm assumes a single layer"
    c_all = jnp.concatenate([c_prompt, c_sample], axis=0)
    layer_w = (w_ada, b_ada, norm_ffn1, w1_gate, w1_up, w1_down, norm_mix, w_in, b_in, w_dw, b_dw,
               ln_conv_g, ln_conv_b, w_conv_out, mh_norm, w_mlstm_out, w_out, norm_ffn2, w2_gate, w2_up, w2_down)
    outs = _layer(x_prompt, x_sample, c_all, state_conv, state_C[0], state_n[0], state_m[0], final_norm,
                  *[w[0] for w in layer_w])
    conv_idx = (2, 6)
    return tuple(o if i < 2 or i in conv_idx else o[None] for i, o in enumerate(outs))
```

```python
import jax
import jax.numpy as jnp
from jax import lax
from jax.experimental import pallas as pl
from jax.experimental.pallas import tpu as pltpu

F32 = jnp.float32
BF16 = jnp.bfloat16

D_MODEL = 1024
N_HEADS = 4
HEAD_DIM = 256
CONV_WIDTH = 31
CONV_STATE = CONV_WIDTH - 1
D_FF = 2816
N_MOD = 9
CHUNK = 128
EPS = 1e-6

LANES = 128
SUBLANES = 8
CONV_PAD = 32
CONV_OFF = CONV_PAD - CONV_STATE
VMEM_LIMIT_BIG = 58 * 1024 * 1024
VMEM_LIMIT_SMALL = 40 * 1024 * 1024


def _sigmoid(x):
    return 1.0 / (1.0 + jnp.exp(-x))


def _log_sigmoid(x):
    return jnp.minimum(x, 0.0) - jnp.log1p(jnp.exp(-jnp.abs(x)))


def _rms(x, g):
    return x * lax.rsqrt(jnp.mean(x * x, axis=-1, keepdims=True) + EPS) * g


def _const_spec(shape):
    zeros = (0,) * len(shape)
    return pl.BlockSpec(shape, lambda *_: zeros, pipeline_mode=pl.Buffered(1))


def _mod_kernel(c_ref, w_ref, b_ref, o_ref):
    c = c_ref[...]
    a = (c * _sigmoid(c)).astype(BF16)
    o_ref[...] = jnp.dot(a, w_ref[...].astype(BF16), preferred_element_type=F32) + b_ref[...]


def _mod_call(c, w_ada, b_ada):
    rows = c.shape[0]
    n_out = w_ada.shape[1]
    bn = n_out // 8
    return pl.pallas_call(
        _mod_kernel,
        grid=(n_out // bn,),
        in_specs=[
            pl.BlockSpec((rows, D_MODEL), lambda j: (0, 0)),
            pl.BlockSpec((D_MODEL, bn), lambda j: (0, j)),
            pl.BlockSpec((1, bn), lambda j: (0, j)),
        ],
        out_specs=pl.BlockSpec((rows, bn), lambda j: (0, j)),
        out_shape=jax.ShapeDtypeStruct((rows, n_out), F32),
        compiler_params=pltpu.CompilerParams(
            dimension_semantics=("arbitrary",), vmem_limit_bytes=VMEM_LIMIT_SMALL),
        name="mod",
    )(c, w_ada, b_ada.reshape(1, n_out))


def _ffn_block(x3, nrm, sh, sc, gt, wg_ref, wu_ref, wd_ref):
    ns, t, d = x3.shape
    h = _rms(x3, nrm) * (1.0 + sc) + sh
    hb = h.reshape(ns * t, d).astype(BF16)
    g = jnp.dot(hb, wg_ref[...], preferred_element_type=F32)
    u = jnp.dot(hb, wu_ref[...], preferred_element_type=F32)
    a = (g * _sigmoid(g) * u).astype(BF16)
    dn = jnp.dot(a, wd_ref[...], preferred_element_type=F32)
    return x3 + (0.5 * gt) * dn.reshape(ns, t, d)


def _in_weight_kernel(wt_ref, wgate_ref, o_ref):
    j = pl.program_id(0)
    n_groups = pl.num_programs(0) - 1

    @pl.when(j < n_groups)
    def _():
        o_ref[...] = wt_ref[...].T.astype(BF16)

    @pl.when(j == n_groups)
    def _():
        gate = wgate_ref[...].T.astype(BF16)
        pad = jnp.zeros((o_ref.shape[0], o_ref.shape[1] - gate.shape[1]), BF16)
        o_ref[...] = jnp.concatenate([gate, pad], axis=1)


def _in_weight_call(w_t, w_gate, n_groups):
    d = w_t.shape[1]
    return pl.pallas_call(
        _in_weight_kernel,
        grid=(n_groups + 1,),
        in_specs=[pl.BlockSpec((d, d), lambda j: (jnp.minimum(j, n_groups - 1), 0)),
                  pl.BlockSpec(w_gate.shape, lambda j: (0, 0))],
        out_specs=pl.BlockSpec((d, d), lambda j: (0, j)),
        out_shape=jax.ShapeDtypeStruct((d, (n_groups + 1) * d), BF16),
        compiler_params=pltpu.CompilerParams(
            dimension_semantics=("arbitrary",), vmem_limit_bytes=VMEM_LIMIT_SMALL),
        name="in_weight",
    )(w_t, w_gate)


def _in_proj(x1, sh2, sc2, n2_ref, wm_ref, bm_ref, glu_ref, q_ref, k_ref, v_ref, og_ref, ga_ref, gb_ref, gt_ref):
    ns, t, d = x1.shape
    u = _rms(x1, n2_ref[...]) * (1.0 + sc2) + sh2
    ub = u.reshape(ns * t, d).astype(BF16)

    def proj(j):
        cols = slice(j * d, (j + 1) * d)
        return jnp.dot(ub, wm_ref[:, cols], preferred_element_type=F32) + bm_ref[:, cols]

    def put(ref, val):
        ref[...] = val.reshape(ns, t, d).astype(ref.dtype)

    put(glu_ref, proj(0) * _sigmoid(proj(1)))
    put(q_ref, proj(2) * (HEAD_DIM ** -0.5))
    put(k_ref, proj(3))
    put(v_ref, proj(4))
    put(og_ref, _sigmoid(proj(5)))
    put(ga_ref, _sigmoid(proj(6)))
    cols = slice(7 * d, 8 * d + LANES)
    last = jnp.dot(ub, wm_ref[:, cols], preferred_element_type=F32) + bm_ref[:, cols]
    put(gb_ref, _sigmoid(last[:, 0:d]))
    gt_ref[...] = last[:, d:].T[0:gt_ref.shape[0], :]


def _ffn_in_kernel(x_ref, mod_ref, n1_ref, wg_ref, wu_ref, wd_ref, n2_ref, wm_ref, bm_ref,
                   x1_ref, *act_refs):
    mod = mod_ref[...]
    sh1, sc1, g1, sh2, sc2 = [mod[:, i:i + 1, :] for i in range(5)]
    x1 = _ffn_block(x_ref[...], n1_ref[...], sh1, sc1, g1, wg_ref, wu_ref, wd_ref)
    x1_ref[...] = x1
    _in_proj(x1, sh2, sc2, n2_ref, wm_ref, bm_ref, *act_refs)


def _ffn_in_call(x, mod_a, nrm1, wg, wu, wd, nrm2, w_main, b_main, *, nseq, t_tile, act_dtype):
    b, s, d = x.shape
    grid = (b // nseq, s // t_tile)
    rows = nseq * t_tile
    tok = pl.BlockSpec((nseq, t_tile, d), lambda i, j: (i, j, 0))
    n_t = s // t_tile
    in_specs = [
        tok,
        pl.BlockSpec((nseq, 5, d), lambda i, j: (i, 0, 0)),
        _const_spec((1, d)),
        _const_spec(wg.shape), _const_spec(wu.shape), _const_spec(wd.shape),
        _const_spec((1, d)),
        _const_spec(w_main.shape), _const_spec(b_main.shape),
    ]
    out_specs = [tok] * 8 + [pl.BlockSpec((2 * N_HEADS, rows), lambda i, j: (0, i * n_t + j))]
    out_shape = ([jax.ShapeDtypeStruct((b, s, d), F32)] * 2
                 + [jax.ShapeDtypeStruct((b, s, d), act_dtype)] * 6
                 + [jax.ShapeDtypeStruct((2 * N_HEADS, b * s), F32)])
    return pl.pallas_call(
        _ffn_in_kernel,
        grid=grid,
        in_specs=in_specs,
        out_specs=out_specs,
        out_shape=out_shape,
        compiler_params=pltpu.CompilerParams(
            dimension_semantics=("arbitrary", "arbitrary"), vmem_limit_bytes=VMEM_LIMIT_BIG),
        name="ffn_in",
    )(x, mod_a, nrm1, wg, wu, wd, nrm2, w_main, b_main)


def _seg_cumsum_lanes(x, seg):
    lane = lax.broadcasted_iota(jnp.int32, x.shape, 1)
    pos = lane & (LANES - 1)
    sh = 1
    while sh < seg:
        x = x + jnp.where(pos >= sh, pltpu.roll(x, sh, 1), 0.0)
        sh *= 2
    return x


def _mlstm_head_chunk(q, k, v, g_row, lf_row, c_st, n_row, m):
    l = q.shape[0]
    t_idx = lax.broadcasted_iota(jnp.int32, (l, l), 0)
    s_idx = lax.broadcasted_iota(jnp.int32, (l, l), 1)
    causal = s_idx <= t_idx
    gb = jnp.broadcast_to(g_row, (l, l))
    lfb = jnp.broadcast_to(lf_row, (l, l))
    cm_col = jnp.max(jnp.where(causal, gb, -jnp.inf), axis=1, keepdims=True)
    bcum_col = jnp.sum(jnp.where(causal, lfb, 0.0), axis=1, keepdims=True)
    g_col = jnp.sum(jnp.where(s_idx == t_idx, gb, 0.0), axis=1, keepdims=True)
    g_max = cm_col[l - 1:l, :]
    qb, kb, vb = q.astype(BF16), k.astype(BF16), v.astype(BF16)

    s = lax.dot_general(qb, kb, (((1,), (1,)), ((), ())), preferred_element_type=F32)
    p = (s * jnp.where(causal, jnp.exp(gb - cm_col), 0.0)).astype(BF16)
    pv = jnp.dot(p, vb, preferred_element_type=F32)
    den_loc = jnp.dot(p, jnp.ones((l, LANES), BF16), preferred_element_type=F32)
    ws_loc = jnp.exp(g_col - g_max)
    vs = (v.astype(F32) * ws_loc).astype(BF16)
    upd = lax.dot_general(vs, kb, (((0,), (0,)), ((), ())), preferred_element_type=F32)
    ws_rows = jnp.broadcast_to(jnp.exp(g_row - g_max), (SUBLANES, l)).astype(BF16)
    n_upd = jnp.dot(ws_rows, kb, preferred_element_type=F32)[0:1, :]

    m_col = jnp.maximum(m, cm_col)
    r_col = jnp.exp(cm_col - m_col)
    w_inter = jnp.exp(m - m_col)
    qc = lax.dot_general(qb, c_st.astype(BF16), (((1,), (1,)), ((), ())), preferred_element_type=F32)
    n_rep = jnp.broadcast_to(n_row, (LANES, n_row.shape[1])).astype(BF16)
    qn = lax.dot_general(qb, n_rep, (((1,), (1,)), ((), ())), preferred_element_type=F32)
    num = r_col * pv + w_inter * qc
    den = jnp.maximum(jnp.abs(r_col * den_loc + w_inter * qn), jnp.exp(-(bcum_col + m_col)))
    inv_den = 1.0 / den
    h = num * jnp.concatenate([inv_den] * (num.shape[1] // LANES), axis=1)
    m_last = m_col[l - 1:l, :]
    a_prev = jnp.exp(m - m_last)
    scale = jnp.exp(g_max - m_last)
    c_new = a_prev * c_st + scale * upd
    n_new = a_prev * n_row + scale * n_upd
    m_new = bcum_col[l - 1:l, :] + m_last
    return h, c_new, n_new, m_new


def _head_out(h, og, mh):
    hn = h * lax.rsqrt(jnp.mean(h * h, axis=-1, keepdims=True) + EPS) * mh
    if og.dtype == BF16:
        return og * hn.astype(BF16)
    return og * hn


CONV_ROW_BLOCK = 8 * SUBLANES


def _conv_prompt_tile(cbuf, xs, wdw_ref, bdw_ref, ycv_ref):
    n_rows, d = cbuf.shape
    t = n_rows - CONV_PAD
    span = n_rows - SUBLANES
    for lb in range(d // LANES):
        ls = slice(lb * LANES, (lb + 1) * LANES)
        xl = xs.at[lb % 2]
        for r in range(1, SUBLANES):
            xl[r, 0:span, :] = cbuf[r:r + span, ls]
        w_l = wdw_ref[:, ls]
        b_l = bdw_ref[:, ls]
        for r0 in range(0, t, CONV_ROW_BLOCK):
            acc = jnp.broadcast_to(b_l, (CONV_ROW_BLOCK, LANES))
            for j in range(CONV_WIDTH):
                r = (CONV_OFF + j) % SUBLANES
                lo = CONV_OFF + j - r + r0
                src = cbuf[lo:lo + CONV_ROW_BLOCK, ls] if r == 0 else xl[r, lo:lo + CONV_ROW_BLOCK, :]
                acc = acc + w_l[j:j + 1, :] * src
            ycv_ref[0, r0:r0 + CONV_ROW_BLOCK, ls] = acc


def _seqmix_prompt_kernel(q_ref, k_ref, v_ref, og_ref, gt_ref, glu_ref, mh_ref, wdw_ref, bdw_ref,
                          hb_ref, ycv_ref, c_out, n_out, m_out, cso_ref, c_s, n_s, m_s, cbuf, xs):
    t = pl.program_id(1)
    tm = q_ref.shape[1]

    @pl.when(t == 0)
    def _():
        c_s[...] = jnp.zeros_like(c_s)
        n_s[...] = jnp.zeros_like(n_s)
        m_s[...] = jnp.zeros_like(m_s)
        cbuf[0:CONV_PAD, :] = jnp.zeros((CONV_PAD, cbuf.shape[1]), F32)

    @pl.when(t > 0)
    def _():
        cbuf[0:CONV_PAD, :] = cbuf[tm:tm + CONV_PAD, :]

    cbuf[CONV_PAD:CONV_PAD + tm, :] = glu_ref[0]
    _conv_prompt_tile(cbuf, xs, wdw_ref, bdw_ref, ycv_ref)
    cso_ref[0] = cbuf[CONV_OFF + tm:CONV_PAD + tm, :]

    g8 = gt_ref[...]
    lf8 = _log_sigmoid(g8)
    bc8 = _seg_cumsum_lanes(lf8, CHUNK)
    lf = lf8[N_HEADS:2 * N_HEADS, :]
    gg = g8[0:N_HEADS, :] - bc8[N_HEADS:2 * N_HEADS, :]
    for c in range(tm // CHUNK):
        rows = slice(c * CHUNK, (c + 1) * CHUNK)
        for hd in range(N_HEADS):
            cols = slice(hd * HEAD_DIM, (hd + 1) * HEAD_DIM)
            h, c_new, n_new, m_new = _mlstm_head_chunk(
                q_ref[0, rows, cols], k_ref[0, rows, cols], v_ref[0, rows, cols],
                gg[hd:hd + 1, rows], lf[hd:hd + 1, rows],
                c_s[hd], n_s[hd:hd + 1, :], m_s[hd:hd + 1, 0:1])
            c_s[hd] = c_new
            n_s[hd:hd + 1, :] = n_new
            m_s[hd:hd + 1, :] = jnp.broadcast_to(m_new, (1, LANES))
            hb_ref[0, rows, cols] = _head_out(h, og_ref[0, rows, cols], mh_ref[:, cols]).astype(hb_ref.dtype)

    @pl.when(t == pl.num_programs(1) - 1)
    def _():
        c_out[0] = c_s[...]
        n_out[0] = n_s[0:N_HEADS, :]
        m_out[0] = m_s[...]


def _seqmix_prompt_call(q, k, v, og, g_t, glu, mh, w_dw, b_dw, *, t_tile):
    b, s, d = q.shape
    n_t = s // t_tile
    tok = pl.BlockSpec((1, t_tile, d), lambda i, j: (i, j, 0))
    return pl.pallas_call(
        _seqmix_prompt_kernel,
        grid=(b, n_t),
        in_specs=[tok, tok, tok, tok,
                  pl.BlockSpec((2 * N_HEADS, t_tile), lambda i, j: (0, i * n_t + j)),
                  tok, _const_spec((1, d)), _const_spec(w_dw.shape), _const_spec((1, d))],
        out_specs=[tok, tok,
                   pl.BlockSpec((1, N_HEADS, HEAD_DIM, HEAD_DIM), lambda i, j: (i, 0, 0, 0)),
                   pl.BlockSpec((1, N_HEADS, HEAD_DIM), lambda i, j: (i, 0, 0)),
                   pl.BlockSpec((1, SUBLANES, LANES), lambda i, j: (i, 0, 0)),
                   pl.BlockSpec((None, 1, CONV_STATE, d), lambda i, j: (0, i, 0, 0))],
        out_shape=[jax.ShapeDtypeStruct((b, s, d), BF16),
                   jax.ShapeDtypeStruct((b, s, d), F32),
                   jax.ShapeDtypeStruct((b, N_HEADS, HEAD_DIM, HEAD_DIM), F32),
                   jax.ShapeDtypeStruct((b, N_HEADS, HEAD_DIM), F32),
                   jax.ShapeDtypeStruct((b, SUBLANES, LANES), F32),
                   jax.ShapeDtypeStruct((1, b, CONV_STATE, d), F32)],
        scratch_shapes=[pltpu.VMEM((N_HEADS, HEAD_DIM, HEAD_DIM), F32),
                        pltpu.VMEM((SUBLANES, HEAD_DIM), F32),
                        pltpu.VMEM((SUBLANES, LANES), F32),
                        pltpu.VMEM((CONV_PAD + t_tile, d), F32),
                        pltpu.VMEM((2, SUBLANES, CONV_PAD + t_tile, LANES), F32)],
        compiler_params=pltpu.CompilerParams(
            dimension_semantics=("arbitrary", "arbitrary"), vmem_limit_bytes=VMEM_LIMIT_SMALL),
        name="seqmix_prompt",
    )(q, k, v, og, g_t, glu, mh, w_dw, b_dw)


def _seqmix_sample_kernel(q_ref, k_ref, v_ref, og_ref, gr_ref, c_in, n_in, m_in, glu_ref, cst_ref,
                          mh_ref, wdw_ref, bdw_ref,
                          hb_ref, ycv_ref, c_out, n_out, m_out, cso_ref, cbuf):
    sb, t, d = q_ref.shape
    cbuf[:, CONV_OFF:CONV_PAD, :] = cst_ref[...]
    cbuf[:, CONV_PAD:CONV_PAD + t, :] = glu_ref[...]
    assert t == SUBLANES
    row_id = lax.broadcasted_iota(jnp.int32, (SUBLANES, LANES), 0)
    for lb in range(d // LANES):
        ls = slice(lb * LANES, (lb + 1) * LANES)
        w_l = wdw_ref[:, ls]
        b_l = jnp.broadcast_to(bdw_ref[:, ls], (SUBLANES, LANES))
        for s in range(sb):
            groups = [cbuf[s, g * SUBLANES:(g + 1) * SUBLANES, ls] for g in range((CONV_PAD + t) // SUBLANES)]
            acc = b_l
            for j in range(CONV_WIDTH):
                g, r = divmod(CONV_OFF + j, SUBLANES)
                if r == 0:
                    win = groups[g]
                else:
                    win = pltpu.roll(jnp.where(row_id >= r, groups[g], groups[g + 1]), SUBLANES - r, 0)
                acc = acc + w_l[j:j + 1, :] * win
            ycv_ref[s, :, ls] = acc
    cso_ref[...] = cbuf[:, CONV_OFF + t:CONV_PAD + t, :]

    lane = lax.broadcasted_iota(jnp.int32, (SUBLANES, LANES), 1)

    for s in range(sb):
        g8 = gr_ref[s]
        lf8 = _log_sigmoid(g8)
        bc8 = _seg_cumsum_lanes(lf8, t)
        lf = lf8[N_HEADS:2 * N_HEADS, :]
        gg = g8[0:N_HEADS, :] - bc8[N_HEADS:2 * N_HEADS, :]
        m_blk = m_in[s]
        m_acc = jnp.zeros((SUBLANES, LANES), F32)
        for hd in range(N_HEADS):
            cols = slice(hd * HEAD_DIM, (hd + 1) * HEAD_DIM)
            h, c_new, n_new, m_new = _mlstm_head_chunk(
                q_ref[s, :, cols], k_ref[s, :, cols], v_ref[s, :, cols],
                gg[hd:hd + 1, 0:t], lf[hd:hd + 1, 0:t],
                c_in[s, hd], n_in[s, hd:hd + 1, :], m_blk[0:1, hd:hd + 1])
            c_out[s, hd] = c_new
            n_out[s, hd:hd + 1, :] = n_new
            m_acc = jnp.where(lane == hd, m_new, m_acc)
            hb_ref[s, :, cols] = _head_out(h, og_ref[s, :, cols], mh_ref[:, cols]).astype(hb_ref.dtype)
        m_out[s] = m_acc


def _seqmix_sample_call(q, k, v, og, g_r, c0, n0, m0, glu, conv_state, mh, w_dw, b_dw, *, seq_block):
    b, t, d = q.shape
    tok = pl.BlockSpec((seq_block, t, d), lambda i: (i, 0, 0))
    small = pl.BlockSpec((seq_block, SUBLANES, LANES), lambda i: (i, 0, 0))
    c_spec = pl.BlockSpec((seq_block, N_HEADS, HEAD_DIM, HEAD_DIM), lambda i: (i, 0, 0, 0))
    n_spec = pl.BlockSpec((seq_block, N_HEADS, HEAD_DIM), lambda i: (i, 0, 0))
    st_spec = pl.BlockSpec((None, seq_block, CONV_STATE, d), lambda i: (0, i, 0, 0))
    return pl.pallas_call(
        _seqmix_sample_kernel,
        grid=(b // seq_block,),
        in_specs=[tok, tok, tok, tok, small, c_spec, n_spec, small, tok, st_spec,
                  _const_spec((1, d)), _const_spec(w_dw.shape), _const_spec((1, d))],
        out_specs=[tok, tok, c_spec, n_spec, small, st_spec],
        out_shape=[jax.ShapeDtypeStruct((b, t, d), F32),
                   jax.ShapeDtypeStruct((b, t, d), F32),
                   jax.ShapeDtypeStruct(c0.shape, F32),
                   jax.ShapeDtypeStruct(n0.shape, F32),
                   jax.ShapeDtypeStruct((b, SUBLANES, LANES), F32),
                   jax.ShapeDtypeStruct((1, b, CONV_STATE, d), F32)],
        scratch_shapes=[pltpu.VMEM((seq_block, CONV_PAD + t, d), F32)],
        compiler_params=pltpu.CompilerParams(
            dimension_semantics=("arbitrary",), vmem_limit_bytes=VMEM_LIMIT_BIG),
        name="seqmix_sample",
    )(q, k, v, og, g_r, c0, n0, m0, glu, conv_state, mh, w_dw, b_dw)


def _mix_tail(x1, yc3, hb, ga, gb, mod, lng_ref, lnb_ref, wco_ref, wmo_ref, wo_ref,
              n3_ref, wg_ref, wu_ref, wd_ref, fn_ref):
    ns, t, d = x1.shape
    rows = ns * t
    yc = yc3.reshape(rows, d)
    mu = jnp.mean(yc, axis=-1, keepdims=True)
    yd = yc - mu
    var = jnp.mean(yd * yd, axis=-1, keepdims=True)
    ln = yd * lax.rsqrt(var + EPS) * lng_ref[...] + lnb_ref[...]
    a_in = (ln * _sigmoid(ln)).astype(BF16)
    a_out = jnp.dot(a_in, wco_ref[...], preferred_element_type=F32)
    b_out = jnp.dot(hb.reshape(rows, d).astype(BF16), wmo_ref[...], preferred_element_type=F32)
    z = (ga.reshape(rows, d).astype(F32) * a_out + gb.reshape(rows, d).astype(F32) * b_out).astype(BF16)
    zo = jnp.dot(z, wo_ref[...], preferred_element_type=F32)
    g2, sh3, sc3, g3 = [mod[:, i:i + 1, :] for i in range(4)]
    x2 = x1 + g2 * zo.reshape(ns, t, d)
    x3 = _ffn_block(x2, n3_ref[...], sh3, sc3, g3, wg_ref, wu_ref, wd_ref)
    return _rms(x3, fn_ref[...])


def _tail_kernel(x1_ref, ycv_ref, hb_ref, ga_ref, gb_ref, mod_ref, *rest):
    *tail_w, y_ref = rest
    y_ref[...] = _mix_tail(x1_ref[...], ycv_ref[...], hb_ref[...], ga_ref[...], gb_ref[...], mod_ref[...], *tail_w)


def _tail_call(x1, ycv, hb, ga, gb, mod_b, weights, *, nseq, t_tile):
    b, s, d = x1.shape
    tok = pl.BlockSpec((nseq, t_tile, d), lambda i, j: (i, j, 0))
    in_specs = ([tok] * 5 + [pl.BlockSpec((nseq, 4, d), lambda i, j: (i, 0, 0))]
                + [_const_spec(w.shape) for w in weights])
    return pl.pallas_call(
        _tail_kernel,
        grid=(b // nseq, s // t_tile),
        in_specs=in_specs,
        out_specs=tok,
        out_shape=jax.ShapeDtypeStruct((b, s, d), F32),
        compiler_params=pltpu.CompilerParams(
            dimension_semantics=("arbitrary", "arbitrary"), vmem_limit_bytes=VMEM_LIMIT_BIG),
        name="tail",
    )(x1, ycv, hb, ga, gb, mod_b, *weights)


def _layer(xp, xs, c_all, st_conv, st_c, st_n, st_m, final_norm,
           w_ada, b_ada, norm_ffn1, w1_gate, w1_up, w1_down, norm_mix, w_in, b_in, w_dw, b_dw,
           ln_conv_g, ln_conv_b, w_conv_out, mh_norm, w_mlstm_out, w_out, norm_ffn2, w2_gate, w2_up, w2_down):
    bp = xp.shape[0]
    bs, ts, d = xs.shape
    row = lambda a: a.reshape(1, -1)

    mod = _mod_call(c_all, w_ada, b_ada).reshape(-1, N_MOD, d)
    mod_a, mod_b = mod[:, 0:5, :], mod[:, 5:N_MOD, :]

    n_main = 8 * d
    assert w_in.shape[1] == n_main + 2 * N_HEADS
    w_t = w_in.T
    w_gate_t = jnp.pad(w_t[n_main:], ((0, LANES - 2 * N_HEADS), (0, 0)))
    w_main = _in_weight_call(w_t, w_gate_t, n_main // d)
    b_main = row(jnp.pad(b_in, (0, w_main.shape[1] - w_in.shape[1])))
    ffn1 = (row(norm_ffn1), w1_gate.astype(BF16), w1_up.astype(BF16), w1_down.astype(BF16))
    in_w = (row(norm_mix), w_main, b_main)
    conv_w = (row(mh_norm), jnp.concatenate([w_dw, jnp.zeros((CONV_PAD - CONV_WIDTH, d), F32)], axis=0), row(b_dw))
    tail_w = (row(ln_conv_g), row(ln_conv_b),
              w_conv_out.astype(BF16), w_mlstm_out.astype(BF16), w_out.astype(BF16),
              row(norm_ffn2), w2_gate.astype(BF16), w2_up.astype(BF16), w2_down.astype(BF16), row(final_norm))

    x1, glu, q, k, v, og, ga, gb, g_t = _ffn_in_call(
        xp, mod_a[:bp], *ffn1, *in_w, nseq=1, t_tile=256, act_dtype=BF16)
    hb, ycv, c_p, n_p, m_p, conv_p = _seqmix_prompt_call(q, k, v, og, g_t, glu, *conv_w, t_tile=512)
    y_p = _tail_call(x1, ycv, hb, ga, gb, mod_b[:bp], tail_w, nseq=1, t_tile=512)

    x1, glu, q, k, v, og, ga, gb, g_t = _ffn_in_call(
        xs, mod_a[bp:], *ffn1, *in_w, nseq=16, t_tile=ts, act_dtype=F32)
    g_r = g_t.reshape(2 * N_HEADS, bs, ts).transpose(1, 0, 2)
    g_r = jnp.pad(g_r, ((0, 0), (0, 0), (0, LANES - ts)))
    m_b = jnp.broadcast_to(jnp.pad(st_m, ((0, 0), (0, LANES - N_HEADS)))[:, None, :], (bs, SUBLANES, LANES))
    hb, ycv, c_s, n_s, m_s, conv_s = _seqmix_sample_call(
        q, k, v, og, g_r, st_c, st_n, m_b, glu, st_conv, *conv_w, seq_block=8)
    y_s = _tail_call(x1, ycv, hb, ga, gb, mod_b[bp:], tail_w, nseq=32, t_tile=ts)

    return (y_p, y_s, conv_p, c_p, n_p, m_p[:, :N_HEADS, 0], conv_s, c_s, n_s, m_s[:, 0, :N_HEADS])


def kernel(x_prompt, x_sample, c_prompt, c_sample, state_conv, state_C, state_n, state_m, w_ada, b_ada, norm_ffn1, w1_gate, w1_up, w1_down, norm_mix, w_in, b_in, w_dw, b_dw, ln_conv_g, ln_conv_b, w_conv_out, mh_norm, w_mlstm_out, w_out, norm_ffn2, w2_gate, w2_up, w2_down, final_norm):
    depth = w_ada.shape[0]
    assert depth == 1, "the fused final norm assumes a single layer"
    c_all = jnp.concatenate([c_prompt, c_sample], axis=0)
    layer_w = (w_ada, b_ada, norm_ffn1, w1_gate, w1_up, w1_down, norm_mix, w_in, b_in, w_dw, b_dw,
               ln_conv_g, ln_conv_b, w_conv_out, mh_norm, w_mlstm_out, w_out, norm_ffn2, w2_gate, w2_up, w2_down)
    outs = _layer(x_prompt, x_sample, c_all, state_conv, state_C[0], state_n[0], state_m[0], final_norm,
                  *[w[0] for w in layer_w])
    conv_idx = (2, 6)
    return tuple(o if i < 2 or i in conv_idx else o[None] for i, o in enumerate(outs))
```

```python
import jax
import jax.numpy as jnp
from jax import lax
from jax.experimental import pallas as pl
from jax.experimental.pallas import tpu as pltpu

F32 = jnp.float32
BF16 = jnp.bfloat16

D_MODEL = 1024
N_HEADS = 4
HEAD_DIM = 256
CONV_WIDTH = 31
CONV_STATE = CONV_WIDTH - 1
D_FF = 2816
N_MOD = 9
CHUNK = 128
EPS = 1e-6

LANES = 128
SUBLANES = 8
CONV_PAD = 32
CONV_OFF = CONV_PAD - CONV_STATE
IN_WEIGHT_BLOCKS = 5
VMEM_LIMIT_BIG = 58 * 1024 * 1024
VMEM_LIMIT_SMALL = 40 * 1024 * 1024


def _sigmoid(x):
    return 1.0 / (1.0 + jnp.exp(-x))


def _log_sigmoid(x):
    return jnp.minimum(x, 0.0) - jnp.log1p(jnp.exp(-jnp.abs(x)))


def _rms(x, g):
    return x * lax.rsqrt(jnp.mean(x * x, axis=-1, keepdims=True) + EPS) * g


def _const_spec(shape):
    zeros = (0,) * len(shape)
    return pl.BlockSpec(shape, lambda *_: zeros, pipeline_mode=pl.Buffered(1))


def _mod_kernel(c_ref, w_ref, b_ref, o_ref):
    c = c_ref[...]
    a = (c * _sigmoid(c)).astype(BF16)
    o_ref[...] = jnp.dot(a, w_ref[...].astype(BF16), preferred_element_type=F32) + b_ref[...]


def _mod_call(c, w_ada, b_ada):
    rows = c.shape[0]
    n_out = w_ada.shape[1]
    bn = n_out // 8
    return pl.pallas_call(
        _mod_kernel,
        grid=(n_out // bn,),
        in_specs=[
            pl.BlockSpec((rows, D_MODEL), lambda j: (0, 0)),
            pl.BlockSpec((D_MODEL, bn), lambda j: (0, j)),
            pl.BlockSpec((1, bn), lambda j: (0, j)),
        ],
        out_specs=pl.BlockSpec((rows, bn), lambda j: (0, j)),
        out_shape=jax.ShapeDtypeStruct((rows, n_out), F32),
        compiler_params=pltpu.CompilerParams(
            dimension_semantics=("arbitrary",), vmem_limit_bytes=VMEM_LIMIT_SMALL),
        name="mod",
    )(c, w_ada, b_ada.reshape(1, n_out))


def _ffn_block(x3, nrm, sh, sc, gt, wg_ref, wu_ref, wd_ref):
    ns, t, d = x3.shape
    h = _rms(x3, nrm) * (1.0 + sc) + sh
    hb = h.reshape(ns * t, d).astype(BF16)
    g = jnp.dot(hb, wg_ref[...], preferred_element_type=F32)
    u = jnp.dot(hb, wu_ref[...], preferred_element_type=F32)
    a = (g * _sigmoid(g) * u).astype(BF16)
    dn = jnp.dot(a, wd_ref[...], preferred_element_type=F32)
    return x3 + (0.5 * gt) * dn.reshape(ns, t, d)


def _in_weight_kernel(wt_ref, wlast_ref, o_ref):
    last = pl.num_programs(0) - 1

    @pl.when(pl.program_id(0) < last)
    def _():
        o_ref[...] = wt_ref[...].T.astype(BF16)

    @pl.when(pl.program_id(0) == last)
    def _():
        o_ref[...] = wlast_ref[...].T.astype(BF16)


def _in_weight_call(w_t, n_blocks):
    n_in, d = w_t.shape
    n_pad = -(-n_in // LANES) * LANES
    blk = n_pad // n_blocks
    assert blk * n_blocks == n_pad and blk % LANES == 0 and (n_blocks - 1) * blk <= n_in
    w_last = jnp.pad(w_t[(n_blocks - 1) * blk:], ((0, n_pad - n_in), (0, 0)))
    return pl.pallas_call(
        _in_weight_kernel,
        grid=(n_blocks,),
        in_specs=[pl.BlockSpec((blk, d), lambda j: (jnp.minimum(j, n_blocks - 2), 0)),
                  pl.BlockSpec((blk, d), lambda j: (0, 0))],
        out_specs=pl.BlockSpec((d, blk), lambda j: (0, j)),
        out_shape=jax.ShapeDtypeStruct((d, n_pad), BF16),
        compiler_params=pltpu.CompilerParams(
            dimension_semantics=("arbitrary",), vmem_limit_bytes=VMEM_LIMIT_BIG),
        name="in_weight",
    )(w_t, w_last)


def _in_proj(x1, sh2, sc2, n2_ref, wm_ref, bm_ref, glu_ref, q_ref, k_ref, v_ref, og_ref, ga_ref, gb_ref, gt_ref):
    ns, t, d = x1.shape
    u = _rms(x1, n2_ref[...]) * (1.0 + sc2) + sh2
    ub = u.reshape(ns * t, d).astype(BF16)

    def proj(j):
        cols = slice(j * d, (j + 1) * d)
        return jnp.dot(ub, wm_ref[:, cols], preferred_element_type=F32) + bm_ref[:, cols]

    def put(ref, val):
        ref[...] = val.reshape(ns, t, d).astype(ref.dtype)

    put(glu_ref, proj(0) * _sigmoid(proj(1)))
    put(q_ref, proj(2) * (HEAD_DIM ** -0.5))
    put(k_ref, proj(3))
    put(v_ref, proj(4))
    put(og_ref, _sigmoid(proj(5)))
    put(ga_ref, _sigmoid(proj(6)))
    put(gb_ref, _sigmoid(proj(7)))
    n_gates = gt_ref.shape[0]
    gates = jnp.dot(ub, wm_ref[:, 8 * d:], preferred_element_type=F32) + bm_ref[:, 8 * d:]
    gt_ref[...] = gates.T[0:n_gates, :]


def _ffn_in_kernel(x_ref, mod_ref, n1_ref, wg_ref, wu_ref, wd_ref, n2_ref, wm_ref, bm_ref,
                   x1_ref, *act_refs):
    mod = mod_ref[...]
    sh1, sc1, g1, sh2, sc2 = [mod[:, i:i + 1, :] for i in range(5)]
    x1 = _ffn_block(x_ref[...], n1_ref[...], sh1, sc1, g1, wg_ref, wu_ref, wd_ref)
    x1_ref[...] = x1
    _in_proj(x1, sh2, sc2, n2_ref, wm_ref, bm_ref, *act_refs)


def _ffn_in_call(x, mod_a, nrm1, wg, wu, wd, nrm2, w_main, b_main, *, nseq, t_tile, act_dtype):
    b, s, d = x.shape
    grid = (b // nseq, s // t_tile)
    rows = nseq * t_tile
    tok = pl.BlockSpec((nseq, t_tile, d), lambda i, j: (i, j, 0))
    n_t = s // t_tile
    in_specs = [
        tok,
        pl.BlockSpec((nseq, 5, d), lambda i, j: (i, 0, 0)),
        _const_spec((1, d)),
        _const_spec(wg.shape), _const_spec(wu.shape), _const_spec(wd.shape),
        _const_spec((1, d)),
        _const_spec(w_main.shape), _const_spec(b_main.shape),
    ]
    out_specs = [tok] * 8 + [pl.BlockSpec((2 * N_HEADS, rows), lambda i, j: (0, i * n_t + j))]
    out_shape = ([jax.ShapeDtypeStruct((b, s, d), F32)] * 2
                 + [jax.ShapeDtypeStruct((b, s, d), act_dtype)] * 6
                 + [jax.ShapeDtypeStruct((2 * N_HEADS, b * s), F32)])
    return pl.pallas_call(
        _ffn_in_kernel,
        grid=grid,
        in_specs=in_specs,
        out_specs=out_specs,
        out_shape=out_shape,
        compiler_params=pltpu.CompilerParams(
            dimension_semantics=("arbitrary", "arbitrary"), vmem_limit_bytes=VMEM_LIMIT_BIG),
        name="ffn_in",
    )(x, mod_a, nrm1, wg, wu, wd, nrm2, w_main, b_main)


def _seg_cumsum_lanes(x, seg):
    lane = lax.broadcasted_iota(jnp.int32, x.shape, 1)
    pos = lane & (LANES - 1)
    sh = 1
    while sh < seg:
        x = x + jnp.where(pos >= sh, pltpu.roll(x, sh, 1), 0.0)
        sh *= 2
    return x


def _mlstm_head_chunk(q, k, v, g_row, lf_row, c_st, n_row, m):
    l = q.shape[0]
    t_idx = lax.broadcasted_iota(jnp.int32, (l, l), 0)
    s_idx = lax.broadcasted_iota(jnp.int32, (l, l), 1)
    causal = s_idx <= t_idx
    gb = jnp.broadcast_to(g_row, (l, l))
    lfb = jnp.broadcast_to(lf_row, (l, l))
    cm_col = jnp.max(jnp.where(causal, gb, -jnp.inf), axis=1, keepdims=True)
    bcum_col = jnp.sum(jnp.where(causal, lfb, 0.0), axis=1, keepdims=True)
    g_col = jnp.sum(jnp.where(s_idx == t_idx, gb, 0.0), axis=1, keepdims=True)
    g_max = cm_col[l - 1:l, :]
    qb, kb, vb = q.astype(BF16), k.astype(BF16), v.astype(BF16)

    s = lax.dot_general(qb, kb, (((1,), (1,)), ((), ())), preferred_element_type=F32)
    p = (s * jnp.where(causal, jnp.exp(gb - cm_col), 0.0)).astype(BF16)
    pv = jnp.dot(p, vb, preferred_element_type=F32)
    den_loc = jnp.dot(p, jnp.ones((l, LANES), BF16), preferred_element_type=F32)
    ws_loc = jnp.exp(g_col - g_max)
    vs = (v.astype(F32) * ws_loc).astype(BF16)
    upd = lax.dot_general(vs, kb, (((0,), (0,)), ((), ())), preferred_element_type=F32)
    ws_rows = jnp.broadcast_to(jnp.exp(g_row - g_max), (SUBLANES, l)).astype(BF16)
    n_upd = jnp.dot(ws_rows, kb, preferred_element_type=F32)[0:1, :]

    m_col = jnp.maximum(m, cm_col)
    r_col = jnp.exp(cm_col - m_col)
    w_inter = jnp.exp(m - m_col)
    qc = lax.dot_general(qb, c_st.astype(BF16), (((1,), (1,)), ((), ())), preferred_element_type=F32)
    n_rep = jnp.broadcast_to(n_row, (LANES, n_row.shape[1])).astype(BF16)
    qn = lax.dot_general(qb, n_rep, (((1,), (1,)), ((), ())), preferred_element_type=F32)
    num = r_col * pv + w_inter * qc
    den = jnp.maximum(jnp.abs(r_col * den_loc + w_inter * qn), jnp.exp(-(bcum_col + m_col)))
    inv_den = 1.0 / den
    h = num * jnp.concatenate([inv_den] * (num.shape[1] // LANES), axis=1)
    m_last = m_col[l - 1:l, :]
    a_prev = jnp.exp(m - m_last)
    scale = jnp.exp(g_max - m_last)
    c_new = a_prev * c_st + scale * upd
    n_new = a_prev * n_row + scale * n_upd
    m_new = bcum_col[l - 1:l, :] + m_last
    return h, c_new, n_new, m_new


def _head_out(h, og, mh):
    hn = h * lax.rsqrt(jnp.mean(h * h, axis=-1, keepdims=True) + EPS) * mh
    if og.dtype == BF16:
        return og * hn.astype(BF16)
    return og * hn


CONV_ROW_BLOCK = 8 * SUBLANES


def _conv_prompt_tile(cbuf, xs, wdw_ref, bdw_ref, ycv_ref):
    n_rows, d = cbuf.shape
    t = n_rows - CONV_PAD
    span = n_rows - SUBLANES
    for lb in range(d // LANES):
        ls = slice(lb * LANES, (lb + 1) * LANES)
        xl = xs.at[lb % 2]
        for r in range(1, SUBLANES):
            xl[r, 0:span, :] = cbuf[r:r + span, ls]
        w_l = wdw_ref[:, ls]
        b_l = bdw_ref[:, ls]
        for r0 in range(0, t, CONV_ROW_BLOCK):
            acc = jnp.broadcast_to(b_l, (CONV_ROW_BLOCK, LANES))
            for j in range(CONV_WIDTH):
                r = (CONV_OFF + j) % SUBLANES
                lo = CONV_OFF + j - r + r0
                src = cbuf[lo:lo + CONV_ROW_BLOCK, ls] if r == 0 else xl[r, lo:lo + CONV_ROW_BLOCK, :]
                acc = acc + w_l[j:j + 1, :] * src
            ycv_ref[0, r0:r0 + CONV_ROW_BLOCK, ls] = acc


def _seqmix_prompt_kernel(q_ref, k_ref, v_ref, og_ref, gt_ref, glu_ref, mh_ref, wdw_ref, bdw_ref,
                          hb_ref, ycv_ref, c_out, n_out, m_out, cso_ref, c_s, n_s, m_s, cbuf, xs):
    t = pl.program_id(1)
    tm = q_ref.shape[1]

    @pl.when(t == 0)
    def _():
        c_s[...] = jnp.zeros_like(c_s)
        n_s[...] = jnp.zeros_like(n_s)
        m_s[...] = jnp.zeros_like(m_s)
        cbuf[0:CONV_PAD, :] = jnp.zeros((CONV_PAD, cbuf.shape[1]), F32)

    @pl.when(t > 0)
    def _():
        cbuf[0:CONV_PAD, :] = cbuf[tm:tm + CONV_PAD, :]

    cbuf[CONV_PAD:CONV_PAD + tm, :] = glu_ref[0]
    _conv_prompt_tile(cbuf, xs, wdw_ref, bdw_ref, ycv_ref)
    cso_ref[0] = cbuf[CONV_OFF + tm:CONV_PAD + tm, :]

    g8 = gt_ref[...]
    lf8 = _log_sigmoid(g8)
    bc8 = _seg_cumsum_lanes(lf8, CHUNK)
    lf = lf8[N_HEADS:2 * N_HEADS, :]
    gg = g8[0:N_HEADS, :] - bc8[N_HEADS:2 * N_HEADS, :]
    for c in range(tm // CHUNK):
        rows = slice(c * CHUNK, (c + 1) * CHUNK)
        for hd in range(N_HEADS):
            cols = slice(hd * HEAD_DIM, (hd + 1) * HEAD_DIM)
            h, c_new, n_new, m_new = _mlstm_head_chunk(
                q_ref[0, rows, cols], k_ref[0, rows, cols], v_ref[0, rows, cols],
                gg[hd:hd + 1, rows], lf[hd:hd + 1, rows],
                c_s[hd], n_s[hd:hd + 1, :], m_s[hd:hd + 1, 0:1])
            c_s[hd] = c_new
            n_s[hd:hd + 1, :] = n_new
            m_s[hd:hd + 1, :] = jnp.broadcast_to(m_new, (1, LANES))
            hb_ref[0, rows, cols] = _head_out(h, og_ref[0, rows, cols], mh_ref[:, cols]).astype(hb_ref.dtype)

    @pl.when(t == pl.num_programs(1) - 1)
    def _():
        c_out[0] = c_s[...]
        n_out[0] = n_s[0:N_HEADS, :]
        m_out[0] = m_s[...]


def _seqmix_prompt_call(q, k, v, og, g_t, glu, mh, w_dw, b_dw, *, t_tile):
    b, s, d = q.shape
    n_t = s // t_tile
    tok = pl.BlockSpec((1, t_tile, d), lambda i, j: (i, j, 0))
    return pl.pallas_call(
        _seqmix_prompt_kernel,
        grid=(b, n_t),
        in_specs=[tok, tok, tok, tok,
                  pl.BlockSpec((2 * N_HEADS, t_tile), lambda i, j: (0, i * n_t + j)),
                  tok, _const_spec((1, d)), _const_spec(w_dw.shape), _const_spec((1, d))],
        out_specs=[tok, tok,
                   pl.BlockSpec((1, N_HEADS, HEAD_DIM, HEAD_DIM), lambda i, j: (i, 0, 0, 0)),
                   pl.BlockSpec((1, N_HEADS, HEAD_DIM), lambda i, j: (i, 0, 0)),
                   pl.BlockSpec((1, SUBLANES, LANES), lambda i, j: (i, 0, 0)),
                   pl.BlockSpec((None, 1, CONV_STATE, d), lambda i, j: (0, i, 0, 0))],
        out_shape=[jax.ShapeDtypeStruct((b, s, d), BF16),
                   jax.ShapeDtypeStruct((b, s, d), F32),
                   jax.ShapeDtypeStruct((b, N_HEADS, HEAD_DIM, HEAD_DIM), F32),
                   jax.ShapeDtypeStruct((b, N_HEADS, HEAD_DIM), F32),
                   jax.ShapeDtypeStruct((b, SUBLANES, LANES), F32),
                   jax.ShapeDtypeStruct((1, b, CONV_STATE, d), F32)],
        scratch_shapes=[pltpu.VMEM((N_HEADS, HEAD_DIM, HEAD_DIM), F32),
                        pltpu.VMEM((SUBLANES, HEAD_DIM), F32),
                        pltpu.VMEM((SUBLANES, LANES), F32),
                        pltpu.VMEM((CONV_PAD + t_tile, d), F32),
                        pltpu.VMEM((2, SUBLANES, CONV_PAD + t_tile, LANES), F32)],
        compiler_params=pltpu.CompilerParams(
            dimension_semantics=("arbitrary", "arbitrary"), vmem_limit_bytes=VMEM_LIMIT_SMALL),
        name="seqmix_prompt",
    )(q, k, v, og, g_t, glu, mh, w_dw, b_dw)


def _seqmix_sample_kernel(q_ref, k_ref, v_ref, og_ref, gr_ref, c_in, n_in, m_in, glu_ref, cst_ref,
                          mh_ref, wdw_ref, bdw_ref,
                          hb_ref, ycv_ref, c_out, n_out, m_out, cso_ref, cbuf):
    sb, t, d = q_ref.shape
    cbuf[:, CONV_OFF:CONV_PAD, :] = cst_ref[...]
    cbuf[:, CONV_PAD:CONV_PAD + t, :] = glu_ref[...]
    assert t == SUBLANES
    row_id = lax.broadcasted_iota(jnp.int32, (SUBLANES, LANES), 0)
    for lb in range(d // LANES):
        ls = slice(lb * LANES, (lb + 1) * LANES)
        w_l = wdw_ref[:, ls]
        b_l = jnp.broadcast_to(bdw_ref[:, ls], (SUBLANES, LANES))
        for s in range(sb):
            groups = [cbuf[s, g * SUBLANES:(g + 1) * SUBLANES, ls] for g in range((CONV_PAD + t) // SUBLANES)]
            acc = b_l
            for j in range(CONV_WIDTH):
                g, r = divmod(CONV_OFF + j, SUBLANES)
                if r == 0:
                    win = groups[g]
                else:
                    win = pltpu.roll(jnp.where(row_id >= r, groups[g], groups[g + 1]), SUBLANES - r, 0)
                acc = acc + w_l[j:j + 1, :] * win
            ycv_ref[s, :, ls] = acc
    cso_ref[...] = cbuf[:, CONV_OFF + t:CONV_PAD + t, :]

    lane = lax.broadcasted_iota(jnp.int32, (SUBLANES, LANES), 1)

    for s in range(sb):
        g8 = gr_ref[s]
        lf8 = _log_sigmoid(g8)
        bc8 = _seg_cumsum_lanes(lf8, t)
        lf = lf8[N_HEADS:2 * N_HEADS, :]
        gg = g8[0:N_HEADS, :] - bc8[N_HEADS:2 * N_HEADS, :]
        m_blk = m_in[s]
        m_acc = jnp.zeros((SUBLANES, LANES), F32)
        for hd in range(N_HEADS):
            cols = slice(hd * HEAD_DIM, (hd + 1) * HEAD_DIM)
            h, c_new, n_new, m_new = _mlstm_head_chunk(
                q_ref[s, :, cols], k_ref[s, :, cols], v_ref[s, :, cols],
                gg[hd:hd + 1, 0:t], lf[hd:hd + 1, 0:t],
                c_in[s, hd], n_in[s, hd:hd + 1, :], m_blk[0:1, hd:hd + 1])
            c_out[s, hd] = c_new
            n_out[s, hd:hd + 1, :] = n_new
            m_acc = jnp.where(lane == hd, m_new, m_acc)
            hb_ref[s, :, cols] = _head_out(h, og_ref[s, :, cols], mh_ref[:, cols]).astype(hb_ref.dtype)
        m_out[s] = m_acc


def _seqmix_sample_call(q, k, v, og, g_r, c0, n0, m0, glu, conv_state, mh, w_dw, b_dw, *, seq_block):
    b, t, d = q.shape
    tok = pl.BlockSpec((seq_block, t, d), lambda i: (i, 0, 0))
    small = pl.BlockSpec((seq_block, SUBLANES, LANES), lambda i: (i, 0, 0))
    c_spec = pl.BlockSpec((seq_block, N_HEADS, HEAD_DIM, HEAD_DIM), lambda i: (i, 0, 0, 0))
    n_spec = pl.BlockSpec((seq_block, N_HEADS, HEAD_DIM), lambda i: (i, 0, 0))
    st_spec = pl.BlockSpec((None, seq_block, CONV_STATE, d), lambda i: (0, i, 0, 0))
    return pl.pallas_call(
        _seqmix_sample_kernel,
        grid=(b // seq_block,),
        in_specs=[tok, tok, tok, tok, small, c_spec, n_spec, small, tok, st_spec,
                  _const_spec((1, d)), _const_spec(w_dw.shape), _const_spec((1, d))],
        out_specs=[tok, tok, c_spec, n_spec, small, st_spec],
        out_shape=[jax.ShapeDtypeStruct((b, t, d), F32),
                   jax.ShapeDtypeStruct((b, t, d), F32),
                   jax.ShapeDtypeStruct(c0.shape, F32),
                   jax.ShapeDtypeStruct(n0.shape, F32),
                   jax.ShapeDtypeStruct((b, SUBLANES, LANES), F32),
                   jax.ShapeDtypeStruct((1, b, CONV_STATE, d), F32)],
        scratch_shapes=[pltpu.VMEM((seq_block, CONV_PAD + t, d), F32)],
        compiler_params=pltpu.CompilerParams(
            dimension_semantics=("arbitrary",), vmem_limit_bytes=VMEM_LIMIT_BIG),
        name="seqmix_sample",
    )(q, k, v, og, g_r, c0, n0, m0, glu, conv_state, mh, w_dw, b_dw)


def _mix_tail(x1, yc3, hb, ga, gb, mod, lng_ref, lnb_ref, wco_ref, wmo_ref, wo_ref,
              n3_ref, wg_ref, wu_ref, wd_ref, fn_ref):
    ns, t, d = x1.shape
    rows = ns * t
    yc = yc3.reshape(rows, d)
    mu = jnp.mean(yc, axis=-1, keepdims=True)
    yd = yc - mu
    var = jnp.mean(yd * yd, axis=-1, keepdims=True)
    ln = yd * lax.rsqrt(var + EPS) * lng_ref[...] + lnb_ref[...]
    a_in = (ln * _sigmoid(ln)).astype(BF16)
    a_out = jnp.dot(a_in, wco_ref[...], preferred_element_type=F32)
    b_out = jnp.dot(hb.reshape(rows, d).astype(BF16), wmo_ref[...], preferred_element_type=F32)
    z = (ga.reshape(rows, d).astype(F32) * a_out + gb.reshape(rows, d).astype(F32) * b_out).astype(BF16)
    zo = jnp.dot(z, wo_ref[...], preferred_element_type=F32)
    g2, sh3, sc3, g3 = [mod[:, i:i + 1, :] for i in range(4)]
    x2 = x1 + g2 * zo.reshape(ns, t, d)
    x3 = _ffn_block(x2, n3_ref[...], sh3, sc3, g3, wg_ref, wu_ref, wd_ref)
    return _rms(x3, fn_ref[...])


def _tail_kernel(x1_ref, ycv_ref, hb_ref, ga_ref, gb_ref, mod_ref, *rest):
    *tail_w, y_ref = rest
    y_ref[...] = _mix_tail(x1_ref[...], ycv_ref[...], hb_ref[...], ga_ref[...], gb_ref[...], mod_ref[...], *tail_w)


def _tail_call(x1, ycv, hb, ga, gb, mod_b, weights, *, nseq, t_tile):
    b, s, d = x1.shape
    tok = pl.BlockSpec((nseq, t_tile, d), lambda i, j: (i, j, 0))
    in_specs = ([tok] * 5 + [pl.BlockSpec((nseq, 4, d), lambda i, j: (i, 0, 0))]
                + [_const_spec(w.shape) for w in weights])
    return pl.pallas_call(
        _tail_kernel,
        grid=(b // nseq, s // t_tile),
        in_specs=in_specs,
        out_specs=tok,
        out_shape=jax.ShapeDtypeStruct((b, s, d), F32),
        compiler_params=pltpu.CompilerParams(
            dimension_semantics=("arbitrary", "arbitrary"), vmem_limit_bytes=VMEM_LIMIT_BIG),
        name="tail",
    )(x1, ycv, hb, ga, gb, mod_b, *weights)


def _layer(xp, xs, c_all, st_conv, st_c, st_n, st_m, final_norm,
           w_ada, b_ada, norm_ffn1, w1_gate, w1_up, w1_down, norm_mix, w_in, b_in, w_dw, b_dw,
           ln_conv_g, ln_conv_b, w_conv_out, mh_norm, w_mlstm_out, w_out, norm_ffn2, w2_gate, w2_up, w2_down):
    bp = xp.shape[0]
    bs, ts, d = xs.shape
    row = lambda a: a.reshape(1, -1)

    mod = _mod_call(c_all, w_ada, b_ada).reshape(-1, N_MOD, d)
    mod_a, mod_b = mod[:, 0:5, :], mod[:, 5:N_MOD, :]

    n_main = 8 * d
    assert w_in.shape[1] == n_main + 2 * N_HEADS
    w_main = _in_weight_call(w_in.T, IN_WEIGHT_BLOCKS)
    b_main = row(jnp.pad(b_in, (0, w_main.shape[1] - w_in.shape[1])))
    ffn1 = (row(norm_ffn1), w1_gate.astype(BF16), w1_up.astype(BF16), w1_down.astype(BF16))
    in_w = (row(norm_mix), w_main, b_main)
    conv_w = (row(mh_norm), jnp.concatenate([w_dw, jnp.zeros((CONV_PAD - CONV_WIDTH, d), F32)], axis=0), row(b_dw))
    tail_w = (row(ln_conv_g), row(ln_conv_b),
              w_conv_out.astype(BF16), w_mlstm_out.astype(BF16), w_out.astype(BF16),
              row(norm_ffn2), w2_gate.astype(BF16), w2_up.astype(BF16), w2_down.astype(BF16), row(final_norm))

    x1, glu, q, k, v, og, ga, gb, g_t = _ffn_in_call(
        xp, mod_a[:bp], *ffn1, *in_w, nseq=1, t_tile=256, act_dtype=BF16)
    hb, ycv, c_p, n_p, m_p, conv_p = _seqmix_prompt_call(q, k, v, og, g_t, glu, *conv_w, t_tile=512)
    y_p = _tail_call(x1, ycv, hb, ga, gb, mod_b[:bp], tail_w, nseq=1, t_tile=512)

    x1, glu, q, k, v, og, ga, gb, g_t = _ffn_in_call(
        xs, mod_a[bp:], *ffn1, *in_w, nseq=32, t_tile=ts, act_dtype=F32)
    g_r = g_t.reshape(2 * N_HEADS, bs, ts).transpose(1, 0, 2)
    g_r = jnp.pad(g_r, ((0, 0), (0, 0), (0, LANES - ts)))
    m_b = jnp.broadcast_to(jnp.pad(st_m, ((0, 0), (0, LANES - N_HEADS)))[:, None, :], (bs, SUBLANES, LANES))
    hb, ycv, c_s, n_s, m_s, conv_s = _seqmix_sample_call(
        q, k, v, og, g_r, st_c, st_n, m_b, glu, st_conv, *conv_w, seq_block=8)
    y_s = _tail_call(x1, ycv, hb, ga, gb, mod_b[bp:], tail_w, nseq=32, t_tile=ts)

    return (y_p, y_s, conv_p, c_p, n_p, m_p[:, :N_HEADS, 0], conv_s, c_s, n_s, m_s[:, 0, :N_HEADS])


def kernel(x_prompt, x_sample, c_prompt, c_sample, state_conv, state_C, state_n, state_m, w_ada, b_ada, norm_ffn1, w1_gate, w1_up, w1_down, norm_mix, w_in, b_in, w_dw, b_dw, ln_conv_g, ln_conv_b, w_conv_out, mh_norm, w_mlstm_out, w_out, norm_ffn2, w2_gate, w2_up, w2_down, final_norm):
    depth = w_ada.shape[0]
    assert depth == 1, "the fused final norm assumes a single layer"
    c_all = jnp.concatenate([c_prompt, c_sample], axis=0)
    layer_w = (w_ada, b_ada, norm_ffn1, w1_gate, w1_up, w1_down, norm_mix, w_in, b_in, w_dw, b_dw,
               ln_conv_g, ln_conv_b, w_conv_out, mh_norm, w_mlstm_out, w_out, norm_ffn2, w2_gate, w2_up, w2_down)
    outs = _layer(x_prompt, x_sample, c_all, state_conv, state_C[0], state_n[0], state_m[0], final_norm,
                  *[w[0] for w in layer_w])
    conv_idx = (2, 6)
    return tuple(o if i < 2 or i in conv_idx else o[None] for i, o in enumerate(outs))
```

```python
import jax
import jax.numpy as jnp
from jax import lax
from jax.experimental import pallas as pl
from jax.experimental.pallas import tpu as pltpu

F32 = jnp.float32
BF16 = jnp.bfloat16

D_MODEL = 1024
N_HEADS = 4
HEAD_DIM = 256
CONV_WIDTH = 31
CONV_STATE = CONV_WIDTH - 1
D_FF = 2816
N_MOD = 9
CHUNK = 128
EPS = 1e-6

LANES = 128
SUBLANES = 8
CONV_PAD = 32
CONV_OFF = CONV_PAD - CONV_STATE
IN_WEIGHT_BLOCKS = 5
VMEM_LIMIT_BIG = 58 * 1024 * 1024
VMEM_LIMIT_SMALL = 40 * 1024 * 1024


def _sigmoid(x):
    return 1.0 / (1.0 + jnp.exp(-x))


def _log_sigmoid(x):
    return jnp.minimum(x, 0.0) - jnp.log1p(jnp.exp(-jnp.abs(x)))


def _rms(x, g):
    return x * lax.rsqrt(jnp.mean(x * x, axis=-1, keepdims=True) + EPS) * g


def _const_spec(shape):
    zeros = (0,) * len(shape)
    return pl.BlockSpec(shape, lambda *_: zeros, pipeline_mode=pl.Buffered(1))


MOD_SPLIT = 5


def _mod_kernel(c_ref, w_ref, b_ref, oa_ref, ob_ref):
    c = c_ref[...]
    a = (c * _sigmoid(c)).astype(BF16)
    val = jnp.dot(a, w_ref[...].astype(BF16), preferred_element_type=F32) + b_ref[...]
    k = pl.program_id(0)
    for i in range(N_MOD):
        ref, row = (oa_ref, i) if i < MOD_SPLIT else (ob_ref, i - MOD_SPLIT)

        @pl.when(k == i)
        def _(ref=ref, row=row):
            ref[:, row, :] = val


def _mod_call(c, w_ada, b_ada):
    rows, d = c.shape
    assert w_ada.shape[1] == N_MOD * d
    return pl.pallas_call(
        _mod_kernel,
        grid=(N_MOD,),
        in_specs=[
            pl.BlockSpec((rows, d), lambda j: (0, 0)),
            pl.BlockSpec((d, d), lambda j: (0, j)),
            pl.BlockSpec((1, d), lambda j: (0, j)),
        ],
        out_specs=[pl.BlockSpec((rows, MOD_SPLIT, d), lambda j: (0, 0, 0)),
                   pl.BlockSpec((rows, N_MOD - MOD_SPLIT, d), lambda j: (0, 0, 0))],
        out_shape=[jax.ShapeDtypeStruct((rows, MOD_SPLIT, d), F32),
                   jax.ShapeDtypeStruct((rows, N_MOD - MOD_SPLIT, d), F32)],
        compiler_params=pltpu.CompilerParams(
            dimension_semantics=("arbitrary",), vmem_limit_bytes=VMEM_LIMIT_SMALL),
        name="mod",
    )(c, w_ada, b_ada.reshape(1, N_MOD * d))


def _ffn_block(x3, nrm, sh, sc, gt, wg_ref, wu_ref, wd_ref):
    ns, t, d = x3.shape
    h = _rms(x3, nrm) * (1.0 + sc) + sh
    hb = h.reshape(ns * t, d).astype(BF16)
    g = jnp.dot(hb, wg_ref[...], preferred_element_type=F32)
    u = jnp.dot(hb, wu_ref[...], preferred_element_type=F32)
    a = (g * _sigmoid(g) * u).astype(BF16)
    dn = jnp.dot(a, wd_ref[...], preferred_element_type=F32)
    return x3 + (0.5 * gt) * dn.reshape(ns, t, d)


def _in_weight_kernel(wt_ref, wlast_ref, o_ref):
    last = pl.num_programs(0) - 1

    @pl.when(pl.program_id(0) < last)
    def _():
        o_ref[...] = wt_ref[...].T.astype(BF16)

    @pl.when(pl.program_id(0) == last)
    def _():
        o_ref[...] = wlast_ref[...].T.astype(BF16)


def _in_weight_call(w_t, n_blocks):
    n_in, d = w_t.shape
    n_pad = -(-n_in // LANES) * LANES
    blk = n_pad // n_blocks
    assert blk * n_blocks == n_pad and blk % LANES == 0 and (n_blocks - 1) * blk <= n_in
    w_last = jnp.pad(w_t[(n_blocks - 1) * blk:], ((0, n_pad - n_in), (0, 0)))
    return pl.pallas_call(
        _in_weight_kernel,
        grid=(n_blocks,),
        in_specs=[pl.BlockSpec((blk, d), lambda j: (jnp.minimum(j, n_blocks - 2), 0)),
                  pl.BlockSpec((blk, d), lambda j: (0, 0))],
        out_specs=pl.BlockSpec((d, blk), lambda j: (0, j)),
        out_shape=jax.ShapeDtypeStruct((d, n_pad), BF16),
        compiler_params=pltpu.CompilerParams(
            dimension_semantics=("arbitrary",), vmem_limit_bytes=VMEM_LIMIT_BIG),
        name="in_weight",
    )(w_t, w_last)


def _in_proj(x1, sh2, sc2, n2_ref, wm_ref, bm_ref, glu_ref, q_ref, k_ref, v_ref, og_ref, ga_ref, gb_ref, gt_ref):
    ns, t, d = x1.shape
    u = _rms(x1, n2_ref[...]) * (1.0 + sc2) + sh2
    ub = u.reshape(ns * t, d).astype(BF16)

    def proj(j):
        cols = slice(j * d, (j + 1) * d)
        return jnp.dot(ub, wm_ref[:, cols], preferred_element_type=F32) + bm_ref[:, cols]

    def put(ref, val):
        ref[...] = val.reshape(ns, t, d).astype(ref.dtype)

    put(glu_ref, proj(0) * _sigmoid(proj(1)))
    put(q_ref, proj(2) * (HEAD_DIM ** -0.5))
    put(k_ref, proj(3))
    put(v_ref, proj(4))
    put(og_ref, _sigmoid(proj(5)))
    put(ga_ref, _sigmoid(proj(6)))
    put(gb_ref, _sigmoid(proj(7)))
    n_gates = gt_ref.shape[0]
    gates = jnp.dot(ub, wm_ref[:, 8 * d:], preferred_element_type=F32) + bm_ref[:, 8 * d:]
    gt_ref[...] = gates.T[0:n_gates, :]


def _ffn_in_kernel(x_ref, mod_ref, n1_ref, wg_ref, wu_ref, wd_ref, n2_ref, wm_ref, bm_ref,
                   x1_ref, *act_refs):
    mod = mod_ref[...]
    sh1, sc1, g1, sh2, sc2 = [mod[:, i:i + 1, :] for i in range(5)]
    x1 = _ffn_block(x_ref[...], n1_ref[...], sh1, sc1, g1, wg_ref, wu_ref, wd_ref)
    x1_ref[...] = x1
    _in_proj(x1, sh2, sc2, n2_ref, wm_ref, bm_ref, *act_refs)


def _ffn_in_call(x, mod_a, nrm1, wg, wu, wd, nrm2, w_main, b_main, *, nseq, t_tile, act_dtype):
    b, s, d = x.shape
    grid = (b // nseq, s // t_tile)
    rows = nseq * t_tile
    tok = pl.BlockSpec((nseq, t_tile, d), lambda i, j: (i, j, 0))
    n_t = s // t_tile
    in_specs = [
        tok,
        pl.BlockSpec((nseq, 5, d), lambda i, j: (i, 0, 0)),
        _const_spec((1, d)),
        _const_spec(wg.shape), _const_spec(wu.shape), _const_spec(wd.shape),
        _const_spec((1, d)),
        _const_spec(w_main.shape), _const_spec(b_main.shape),
    ]
    out_specs = [tok] * 8 + [pl.BlockSpec((2 * N_HEADS, rows), lambda i, j: (0, i * n_t + j))]
    out_shape = ([jax.ShapeDtypeStruct((b, s, d), F32)] * 2
                 + [jax.ShapeDtypeStruct((b, s, d), act_dtype)] * 6
                 + [jax.ShapeDtypeStruct((2 * N_HEADS, b * s), F32)])
    return pl.pallas_call(
        _ffn_in_kernel,
        grid=grid,
        in_specs=in_specs,
        out_specs=out_specs,
        out_shape=out_shape,
        compiler_params=pltpu.CompilerParams(
            dimension_semantics=("arbitrary", "arbitrary"), vmem_limit_bytes=VMEM_LIMIT_BIG),
        name="ffn_in",
    )(x, mod_a, nrm1, wg, wu, wd, nrm2, w_main, b_main)


def _seg_cumsum_lanes(x, seg):
    lane = lax.broadcasted_iota(jnp.int32, x.shape, 1)
    pos = lane & (LANES - 1)
    sh = 1
    while sh < seg:
        x = x + jnp.where(pos >= sh, pltpu.roll(x, sh, 1), 0.0)
        sh *= 2
    return x


def _mlstm_head_chunk(q, k, v, g_row, lf_row, c_st, n_row, m):
    l = q.shape[0]
    t_idx = lax.broadcasted_iota(jnp.int32, (l, l), 0)
    s_idx = lax.broadcasted_iota(jnp.int32, (l, l), 1)
    causal = s_idx <= t_idx
    gb = jnp.broadcast_to(g_row, (l, l))
    lfb = jnp.broadcast_to(lf_row, (l, l))
    cm_col = jnp.max(jnp.where(causal, gb, -jnp.inf), axis=1, keepdims=True)
    bcum_col = jnp.sum(jnp.where(causal, lfb, 0.0), axis=1, keepdims=True)
    g_col = jnp.sum(jnp.where(s_idx == t_idx, gb, 0.0), axis=1, keepdims=True)
    g_max = cm_col[l - 1:l, :]
    qb, kb, vb = q.astype(BF16), k.astype(BF16), v.astype(BF16)

    s = lax.dot_general(qb, kb, (((1,), (1,)), ((), ())), preferred_element_type=F32)
    p = (s * jnp.where(causal, jnp.exp(gb - cm_col), 0.0)).astype(BF16)
    pv = jnp.dot(p, vb, preferred_element_type=F32)
    den_loc = jnp.dot(p, jnp.ones((l, LANES), BF16), preferred_element_type=F32)
    ws_loc = jnp.exp(g_col - g_max)
    vs = (v.astype(F32) * ws_loc).astype(BF16)
    upd = lax.dot_general(vs, kb, (((0,), (0,)), ((), ())), preferred_element_type=F32)
    ws_rows = jnp.broadcast_to(jnp.exp(g_row - g_max), (SUBLANES, l)).astype(BF16)
    n_upd = jnp.dot(ws_rows, kb, preferred_element_type=F32)[0:1, :]

    m_col = jnp.maximum(m, cm_col)
    r_col = jnp.exp(cm_col - m_col)
    w_inter = jnp.exp(m - m_col)
    qc = lax.dot_general(qb, c_st.astype(BF16), (((1,), (1,)), ((), ())), preferred_element_type=F32)
    n_rep = jnp.broadcast_to(n_row, (LANES, n_row.shape[1])).astype(BF16)
    qn = lax.dot_general(qb, n_rep, (((1,), (1,)), ((), ())), preferred_element_type=F32)
    num = r_col * pv + w_inter * qc
    den = jnp.maximum(jnp.abs(r_col * den_loc + w_inter * qn), jnp.exp(-(bcum_col + m_col)))
    inv_den = 1.0 / den
    h = num * jnp.concatenate([inv_den] * (num.shape[1] // LANES), axis=1)
    m_last = m_col[l - 1:l, :]
    a_prev = jnp.exp(m - m_last)
    scale = jnp.exp(g_max - m_last)
    c_new = a_prev * c_st + scale * upd
    n_new = a_prev * n_row + scale * n_upd
    m_new = bcum_col[l - 1:l, :] + m_last
    return h, c_new, n_new, m_new


def _head_out(h, og, mh):
    hn = h * lax.rsqrt(jnp.mean(h * h, axis=-1, keepdims=True) + EPS) * mh
    if og.dtype == BF16:
        return og * hn.astype(BF16)
    return og * hn


CONV_ROW_BLOCK = 8 * SUBLANES


def _conv_prompt_tile(cbuf, xs, wdw_ref, bdw_ref, ycv_ref):
    n_rows, d = cbuf.shape
    t = n_rows - CONV_PAD
    span = n_rows - SUBLANES
    for lb in range(d // LANES):
        ls = slice(lb * LANES, (lb + 1) * LANES)
        xl = xs.at[lb % 2]
        for r in range(1, SUBLANES):
            xl[r, 0:span, :] = cbuf[r:r + span, ls]
        w_l = wdw_ref[:, ls]
        b_l = bdw_ref[:, ls]
        for r0 in range(0, t, CONV_ROW_BLOCK):
            acc = jnp.broadcast_to(b_l, (CONV_ROW_BLOCK, LANES))
            for j in range(CONV_WIDTH):
                r = (CONV_OFF + j) % SUBLANES
                lo = CONV_OFF + j - r + r0
                src = cbuf[lo:lo + CONV_ROW_BLOCK, ls] if r == 0 else xl[r, lo:lo + CONV_ROW_BLOCK, :]
                acc = acc + w_l[j:j + 1, :] * src
            ycv_ref[0, r0:r0 + CONV_ROW_BLOCK, ls] = acc


def _seqmix_prompt_kernel(q_ref, k_ref, v_ref, og_ref, gt_ref, glu_ref, mh_ref, wdw_ref, bdw_ref,
                          hb_ref, ycv_ref, c_out, n_out, m_out, cso_ref, c_s, n_s, m_s, cbuf, xs):
    t = pl.program_id(1)
    tm = q_ref.shape[1]

    @pl.when(t == 0)
    def _():
        c_s[...] = jnp.zeros_like(c_s)
        n_s[...] = jnp.zeros_like(n_s)
        m_s[...] = jnp.zeros_like(m_s)
        cbuf[0:CONV_PAD, :] = jnp.zeros((CONV_PAD, cbuf.shape[1]), F32)

    @pl.when(t > 0)
    def _():
        cbuf[0:CONV_PAD, :] = cbuf[tm:tm + CONV_PAD, :]

    cbuf[CONV_PAD:CONV_PAD + tm, :] = glu_ref[0]
    _conv_prompt_tile(cbuf, xs, wdw_ref, bdw_ref, ycv_ref)
    cso_ref[0] = cbuf[CONV_OFF + tm:CONV_PAD + tm, :]

    g8 = gt_ref[...]
    lf8 = _log_sigmoid(g8)
    bc8 = _seg_cumsum_lanes(lf8, CHUNK)
    lf = lf8[N_HEADS:2 * N_HEADS, :]
    gg = g8[0:N_HEADS, :] - bc8[N_HEADS:2 * N_HEADS, :]
    for c in range(tm // CHUNK):
        rows = slice(c * CHUNK, (c + 1) * CHUNK)
        for hd in range(N_HEADS):
            cols = slice(hd * HEAD_DIM, (hd + 1) * HEAD_DIM)
            h, c_new, n_new, m_new = _mlstm_head_chunk(
                q_ref[0, rows, cols], k_ref[0, rows, cols], v_ref[0, rows, cols],
                gg[hd:hd + 1, rows], lf[hd:hd + 1, rows],
                c_s[hd], n_s[hd:hd + 1, :], m_s[hd:hd + 1, 0:1])
            c_s[hd] = c_new
            n_s[hd:hd + 1, :] = n_new
            m_s[hd:hd + 1, :] = jnp.broadcast_to(m_new, (1, LANES))
            hb_ref[0, rows, cols] = _head_out(h, og_ref[0, rows, cols], mh_ref[:, cols]).astype(hb_ref.dtype)

    @pl.when(t == pl.num_programs(1) - 1)
    def _():
        c_out[0] = c_s[...]
        n_out[0] = n_s[0:N_HEADS, :]
        m_out[0] = m_s[...]


def _seqmix_prompt_call(q, k, v, og, g_t, glu, mh, w_dw, b_dw, *, t_tile):
    b, s, d = q.shape
    n_t = s // t_tile
    tok = pl.BlockSpec((1, t_tile, d), lambda i, j: (i, j, 0))
    return pl.pallas_call(
        _seqmix_prompt_kernel,
        grid=(b, n_t),
        in_specs=[tok, tok, tok, tok,
                  pl.BlockSpec((2 * N_HEADS, t_tile), lambda i, j: (0, i * n_t + j)),
                  tok, _const_spec((1, d)), _const_spec(w_dw.shape), _const_spec((1, d))],
        out_specs=[tok, tok,
                   pl.BlockSpec((1, N_HEADS, HEAD_DIM, HEAD_DIM), lambda i, j: (i, 0, 0, 0)),
                   pl.BlockSpec((1, N_HEADS, HEAD_DIM), lambda i, j: (i, 0, 0)),
                   pl.BlockSpec((1, SUBLANES, LANES), lambda i, j: (i, 0, 0)),
                   pl.BlockSpec((None, 1, CONV_STATE, d), lambda i, j: (0, i, 0, 0))],
        out_shape=[jax.ShapeDtypeStruct((b, s, d), BF16),
                   jax.ShapeDtypeStruct((b, s, d), F32),
                   jax.ShapeDtypeStruct((b, N_HEADS, HEAD_DIM, HEAD_DIM), F32),
                   jax.ShapeDtypeStruct((b, N_HEADS, HEAD_DIM), F32),
                   jax.ShapeDtypeStruct((b, SUBLANES, LANES), F32),
                   jax.ShapeDtypeStruct((1, b, CONV_STATE, d), F32)],
        scratch_shapes=[pltpu.VMEM((N_HEADS, HEAD_DIM, HEAD_DIM), F32),
                        pltpu.VMEM((SUBLANES, HEAD_DIM), F32),
                        pltpu.VMEM((SUBLANES, LANES), F32),
                        pltpu.VMEM((CONV_PAD + t_tile, d), F32),
                        pltpu.VMEM((2, SUBLANES, CONV_PAD + t_tile, LANES), F32)],
        compiler_params=pltpu.CompilerParams(
            dimension_semantics=("arbitrary", "arbitrary"), vmem_limit_bytes=VMEM_LIMIT_SMALL),
        name="seqmix_prompt",
    )(q, k, v, og, g_t, glu, mh, w_dw, b_dw)


def _seqmix_sample_kernel(q_ref, k_ref, v_ref, og_ref, gr_ref, c_in, n_in, m_in, glu_ref, cst_ref,
                          mh_ref, wdw_ref, bdw_ref,
                          hb_ref, ycv_ref, c_out, n_out, m_out, cso_ref, cbuf):
    sb, t, d = q_ref.shape
    cbuf[:, CONV_OFF:CONV_PAD, :] = cst_ref[...]
    cbuf[:, CONV_PAD:CONV_PAD + t, :] = glu_ref[...]
    assert t == SUBLANES
    row_id = lax.broadcasted_iota(jnp.int32, (SUBLANES, LANES), 0)
    for lb in range(d // LANES):
        ls = slice(lb * LANES, (lb + 1) * LANES)
        w_l = wdw_ref[:, ls]
        b_l = jnp.broadcast_to(bdw_ref[:, ls], (SUBLANES, LANES))
        for s in range(sb):
            groups = [cbuf[s, g * SUBLANES:(g + 1) * SUBLANES, ls] for g in range((CONV_PAD + t) // SUBLANES)]
            acc = b_l
            for j in range(CONV_WIDTH):
                g, r = divmod(CONV_OFF + j, SUBLANES)
                if r == 0:
                    win = groups[g]
                else:
                    win = pltpu.roll(jnp.where(row_id >= r, groups[g], groups[g + 1]), SUBLANES - r, 0)
                acc = acc + w_l[j:j + 1, :] * win
            ycv_ref[s, :, ls] = acc
    cso_ref[...] = cbuf[:, CONV_OFF + t:CONV_PAD + t, :]

    lane = lax.broadcasted_iota(jnp.int32, (SUBLANES, LANES), 1)

    for s in range(sb):
        g8 = gr_ref[s]
        lf8 = _log_sigmoid(g8)
        bc8 = _seg_cumsum_lanes(lf8, t)
        lf = lf8[N_HEADS:2 * N_HEADS, :]
        gg = g8[0:N_HEADS, :] - bc8[N_HEADS:2 * N_HEADS, :]
        m_blk = m_in[s]
        m_acc = jnp.zeros((SUBLANES, LANES), F32)
        for hd in range(N_HEADS):
            cols = slice(hd * HEAD_DIM, (hd + 1) * HEAD_DIM)
            h, c_new, n_new, m_new = _mlstm_head_chunk(
                q_ref[s, :, cols], k_ref[s, :, cols], v_ref[s, :, cols],
                gg[hd:hd + 1, 0:t], lf[hd:hd + 1, 0:t],
                c_in[s, hd], n_in[s, hd:hd + 1, :], m_blk[0:1, hd:hd + 1])
            c_out[s, hd] = c_new
            n_out[s, hd:hd + 1, :] = n_new
            m_acc = jnp.where(lane == hd, m_new, m_acc)
            hb_ref[s, :, cols] = _head_out(h, og_ref[s, :, cols], mh_ref[:, cols]).astype(hb_ref.dtype)
        m_out[s] = m_acc


def _seqmix_sample_call(q, k, v, og, g_r, c0, n0, m0, glu, conv_state, mh, w_dw, b_dw, *, seq_block):
    b, t, d = q.shape
    tok = pl.BlockSpec((seq_block, t, d), lambda i: (i, 0, 0))
    small = pl.BlockSpec((seq_block, SUBLANES, LANES), lambda i: (i, 0, 0))
    c_spec = pl.BlockSpec((seq_block, N_HEADS, HEAD_DIM, HEAD_DIM), lambda i: (i, 0, 0, 0))
    n_spec = pl.BlockSpec((seq_block, N_HEADS, HEAD_DIM), lambda i: (i, 0, 0))
    st_spec = pl.BlockSpec((None, seq_block, CONV_STATE, d), lambda i: (0, i, 0, 0))
    return pl.pallas_call(
        _seqmix_sample_kernel,
        grid=(b // seq_block,),
        in_specs=[tok, tok, tok, tok, small, c_spec, n_spec, small, tok, st_spec,
                  _const_spec((1, d)), _const_spec(w_dw.shape), _const_spec((1, d))],
        out_specs=[tok, tok, c_spec, n_spec, small, st_spec],
        out_shape=[jax.ShapeDtypeStruct((b, t, d), F32),
                   jax.ShapeDtypeStruct((b, t, d), F32),
                   jax.ShapeDtypeStruct(c0.shape, F32),
                   jax.ShapeDtypeStruct(n0.shape, F32),
                   jax.ShapeDtypeStruct((b, SUBLANES, LANES), F32),
                   jax.ShapeDtypeStruct((1, b, CONV_STATE, d), F32)],
        scratch_shapes=[pltpu.VMEM((seq_block, CONV_PAD + t, d), F32)],
        compiler_params=pltpu.CompilerParams(
            dimension_semantics=("arbitrary",), vmem_limit_bytes=VMEM_LIMIT_BIG),
        name="seqmix_sample",
    )(q, k, v, og, g_r, c0, n0, m0, glu, conv_state, mh, w_dw, b_dw)


def _mix_tail(x1, yc3, hb, ga, gb, mod, lng_ref, lnb_ref, wco_ref, wmo_ref, wo_ref,
              n3_ref, wg_ref, wu_ref, wd_ref, fn_ref):
    ns, t, d = x1.shape
    rows = ns * t
    yc = yc3.reshape(rows, d)
    mu = jnp.mean(yc, axis=-1, keepdims=True)
    yd = yc - mu
    var = jnp.mean(yd * yd, axis=-1, keepdims=True)
    ln = yd * lax.rsqrt(var + EPS) * lng_ref[...] + lnb_ref[...]
    a_in = (ln * _sigmoid(ln)).astype(BF16)
    a_out = jnp.dot(a_in, wco_ref[...], preferred_element_type=F32)
    b_out = jnp.dot(hb.reshape(rows, d).astype(BF16), wmo_ref[...], preferred_element_type=F32)
    z = (ga.reshape(rows, d).astype(F32) * a_out + gb.reshape(rows, d).astype(F32) * b_out).astype(BF16)
    zo = jnp.dot(z, wo_ref[...], preferred_element_type=F32)
    g2, sh3, sc3, g3 = [mod[:, i:i + 1, :] for i in range(4)]
    x2 = x1 + g2 * zo.reshape(ns, t, d)
    x3 = _ffn_block(x2, n3_ref[...], sh3, sc3, g3, wg_ref, wu_ref, wd_ref)
    return _rms(x3, fn_ref[...])


def _tail_kernel(x1_ref, ycv_ref, hb_ref, ga_ref, gb_ref, mod_ref, *rest):
    *tail_w, y_ref = rest
    y_ref[...] = _mix_tail(x1_ref[...], ycv_ref[...], hb_ref[...], ga_ref[...], gb_ref[...], mod_ref[...], *tail_w)


def _tail_call(x1, ycv, hb, ga, gb, mod_b, weights, *, nseq, t_tile):
    b, s, d = x1.shape
    tok = pl.BlockSpec((nseq, t_tile, d), lambda i, j: (i, j, 0))
    in_specs = ([tok] * 5 + [pl.BlockSpec((nseq, 4, d), lambda i, j: (i, 0, 0))]
                + [_const_spec(w.shape) for w in weights])
    return pl.pallas_call(
        _tail_kernel,
        grid=(b // nseq, s // t_tile),
        in_specs=in_specs,
        out_specs=tok,
        out_shape=jax.ShapeDtypeStruct((b, s, d), F32),
        compiler_params=pltpu.CompilerParams(
            dimension_semantics=("arbitrary", "arbitrary"), vmem_limit_bytes=VMEM_LIMIT_BIG),
        name="tail",
    )(x1, ycv, hb, ga, gb, mod_b, *weights)


def _layer(xp, xs, c_all, st_conv, st_c, st_n, st_m, final_norm,
           w_ada, b_ada, norm_ffn1, w1_gate, w1_up, w1_down, norm_mix, w_in, b_in, w_dw, b_dw,
           ln_conv_g, ln_conv_b, w_conv_out, mh_norm, w_mlstm_out, w_out, norm_ffn2, w2_gate, w2_up, w2_down):
    bp = xp.shape[0]
    bs, ts, d = xs.shape
    row = lambda a: a.reshape(1, -1)

    mod_a, mod_b = _mod_call(c_all, w_ada, b_ada)

    n_main = 8 * d
    assert w_in.shape[1] == n_main + 2 * N_HEADS
    w_main = _in_weight_call(w_in.T, IN_WEIGHT_BLOCKS)
    b_main = row(jnp.pad(b_in, (0, w_main.shape[1] - w_in.shape[1])))
    ffn1 = (row(norm_ffn1), w1_gate.astype(BF16), w1_up.astype(BF16), w1_down.astype(BF16))
    in_w = (row(norm_mix), w_main, b_main)
    conv_w = (row(mh_norm), jnp.concatenate([w_dw, jnp.zeros((CONV_PAD - CONV_WIDTH, d), F32)], axis=0), row(b_dw))
    tail_w = (row(ln_conv_g), row(ln_conv_b),
              w_conv_out.astype(BF16), w_mlstm_out.astype(BF16), w_out.astype(BF16),
              row(norm_ffn2), w2_gate.astype(BF16), w2_up.astype(BF16), w2_down.astype(BF16), row(final_norm))

    x1, glu, q, k, v, og, ga, gb, g_t = _ffn_in_call(
        xp, mod_a[:bp], *ffn1, *in_w, nseq=1, t_tile=256, act_dtype=BF16)
    hb, ycv, c_p, n_p, m_p, conv_p = _seqmix_prompt_call(q, k, v, og, g_t, glu, *conv_w, t_tile=512)
    y_p = _tail_call(x1, ycv, hb, ga, gb, mod_b[:bp], tail_w, nseq=1, t_tile=512)

    x1, glu, q, k, v, og, ga, gb, g_t = _ffn_in_call(
        xs, mod_a[bp:], *ffn1, *in_w, nseq=32, t_tile=ts, act_dtype=F32)
    g_r = g_t.reshape(2 * N_HEADS, bs, ts).transpose(1, 0, 2)
    g_r = jnp.pad(g_r, ((0, 0), (0, 0), (0, LANES - ts)))
    m_b = jnp.broadcast_to(jnp.pad(st_m, ((0, 0), (0, LANES - N_HEADS)))[:, None, :], (bs, SUBLANES, LANES))
    hb, ycv, c_s, n_s, m_s, conv_s = _seqmix_sample_call(
        q, k, v, og, g_r, st_c, st_n, m_b, glu, st_conv, *conv_w, seq_block=8)
    y_s = _tail_call(x1, ycv, hb, ga, gb, mod_b[bp:], tail_w, nseq=32, t_tile=ts)

    return (y_p, y_s, conv_p, c_p, n_p, m_p[:, :N_HEADS, 0], conv_s, c_s, n_s, m_s[:, 0, :N_HEADS])


def kernel(x_prompt, x_sample, c_prompt, c_sample, state_conv, state_C, state_n, state_m, w_ada, b_ada, norm_ffn1, w1_gate, w1_up, w1_down, norm_mix, w_in, b_in, w_dw, b_dw, ln_conv_g, ln_conv_b, w_conv_out, mh_norm, w_mlstm_out, w_out, norm_ffn2, w2_gate, w2_up, w2_down, final_norm):
    depth = w_ada.shape[0]
    assert depth == 1, "the fused final norm assumes a single layer"
    c_all = jnp.concatenate([c_prompt, c_sample], axis=0)
    layer_w = (w_ada, b_ada, norm_ffn1, w1_gate, w1_up, w1_down, norm_mix, w_in, b_in, w_dw, b_dw,
               ln_conv_g, ln_conv_b, w_conv_out, mh_norm, w_mlstm_out, w_out, norm_ffn2, w2_gate, w2_up, w2_down)
    outs = _layer(x_prompt, x_sample, c_all, state_conv, state_C[0], state_n[0], state_m[0], final_norm,
                  *[w[0] for w in layer_w])
    conv_idx = (2, 6)
    return tuple(o if i < 2 or i in conv_idx else o[None] for i, o in enumerate(outs))
```

```python
import jax
import jax.numpy as jnp
from jax import lax
from jax.experimental import pallas as pl
from jax.experimental.pallas import tpu as pltpu

F32 = jnp.float32
BF16 = jnp.bfloat16

D_MODEL = 1024
N_HEADS = 4
HEAD_DIM = 256
CONV_WIDTH = 31
CONV_STATE = CONV_WIDTH - 1
D_FF = 2816
N_MOD = 9
CHUNK = 128
EPS = 1e-6

LANES = 128
SUBLANES = 8
CONV_PAD = 32
CONV_OFF = CONV_PAD - CONV_STATE
IN_WEIGHT_BLOCKS = 5
VMEM_LIMIT_BIG = 58 * 1024 * 1024
VMEM_LIMIT_SMALL = 40 * 1024 * 1024


def _sigmoid(x):
    return 1.0 / (1.0 + jnp.exp(-x))


def _log_sigmoid(x):
    return jnp.minimum(x, 0.0) - jnp.log1p(jnp.exp(-jnp.abs(x)))


def _rms(x, g):
    return x * lax.rsqrt(jnp.mean(x * x, axis=-1, keepdims=True) + EPS) * g


def _const_spec(shape):
    zeros = (0,) * len(shape)
    return pl.BlockSpec(shape, lambda *_: zeros, pipeline_mode=pl.Buffered(1))


MOD_SPLIT = 5


def _mod_kernel(c_ref, w_ref, b_ref, oa_ref, ob_ref):
    c = c_ref[...]
    a = (c * _sigmoid(c)).astype(BF16)
    val = jnp.dot(a, w_ref[...].astype(BF16), preferred_element_type=F32) + b_ref[...]
    k = pl.program_id(0)
    for i in range(N_MOD):
        ref, row = (oa_ref, i) if i < MOD_SPLIT else (ob_ref, i - MOD_SPLIT)

        @pl.when(k == i)
        def _(ref=ref, row=row):
            ref[:, row, :] = val


def _mod_call(c, w_ada, b_ada):
    rows, d = c.shape
    assert w_ada.shape[1] == N_MOD * d
    return pl.pallas_call(
        _mod_kernel,
        grid=(N_MOD,),
        in_specs=[
            pl.BlockSpec((rows, d), lambda j: (0, 0)),
            pl.BlockSpec((d, d), lambda j: (0, j)),
            pl.BlockSpec((1, d), lambda j: (0, j)),
        ],
        out_specs=[pl.BlockSpec((rows, MOD_SPLIT, d), lambda j: (0, 0, 0)),
                   pl.BlockSpec((rows, N_MOD - MOD_SPLIT, d), lambda j: (0, 0, 0))],
        out_shape=[jax.ShapeDtypeStruct((rows, MOD_SPLIT, d), F32),
                   jax.ShapeDtypeStruct((rows, N_MOD - MOD_SPLIT, d), F32)],
        compiler_params=pltpu.CompilerParams(
            dimension_semantics=("arbitrary",), vmem_limit_bytes=VMEM_LIMIT_SMALL),
        name="mod",
    )(c, w_ada, b_ada.reshape(1, N_MOD * d))


def _ffn_block(x3, nrm, sh, sc, gt, wg_ref, wu_ref, wd_ref):
    ns, t, d = x3.shape
    h = _rms(x3, nrm) * (1.0 + sc) + sh
    hb = h.reshape(ns * t, d).astype(BF16)
    g = jnp.dot(hb, wg_ref[...], preferred_element_type=F32)
    u = jnp.dot(hb, wu_ref[...], preferred_element_type=F32)
    a = (g * _sigmoid(g) * u).astype(BF16)
    dn = jnp.dot(a, wd_ref[...], preferred_element_type=F32)
    return x3 + (0.5 * gt) * dn.reshape(ns, t, d)


def _in_weight_kernel(wt_ref, wlast_ref, o_ref):
    last = pl.num_programs(0) - 1

    @pl.when(pl.program_id(0) < last)
    def _():
        o_ref[...] = wt_ref[...].T.astype(BF16)

    @pl.when(pl.program_id(0) == last)
    def _():
        o_ref[...] = wlast_ref[...].T.astype(BF16)


def _in_weight_call(w_t, n_blocks):
    n_in, d = w_t.shape
    n_pad = -(-n_in // LANES) * LANES
    blk = n_pad // n_blocks
    assert blk * n_blocks == n_pad and blk % LANES == 0 and (n_blocks - 1) * blk <= n_in
    w_last = jnp.pad(w_t[(n_blocks - 1) * blk:], ((0, n_pad - n_in), (0, 0)))
    return pl.pallas_call(
        _in_weight_kernel,
        grid=(n_blocks,),
        in_specs=[pl.BlockSpec((blk, d), lambda j: (jnp.minimum(j, n_blocks - 2), 0)),
                  pl.BlockSpec((blk, d), lambda j: (0, 0))],
        out_specs=pl.BlockSpec((d, blk), lambda j: (0, j)),
        out_shape=jax.ShapeDtypeStruct((d, n_pad), BF16),
        compiler_params=pltpu.CompilerParams(
            dimension_semantics=("arbitrary",), vmem_limit_bytes=VMEM_LIMIT_BIG),
        name="in_weight",
    )(w_t, w_last)


def _in_proj(x1, sh2, sc2, n2_ref, wm_ref, bm_ref, glu_ref, q_ref, k_ref, v_ref, og_ref, ga_ref, gb_ref, gt_ref):
    ns, t, d = x1.shape
    u = _rms(x1, n2_ref[...]) * (1.0 + sc2) + sh2
    ub = u.reshape(ns * t, d).astype(BF16)

    def proj(j):
        cols = slice(j * d, (j + 1) * d)
        return jnp.dot(ub, wm_ref[:, cols], preferred_element_type=F32) + bm_ref[:, cols]

    def put(ref, val):
        ref[...] = val.reshape(ns, t, d).astype(ref.dtype)

    put(glu_ref, proj(0) * _sigmoid(proj(1)))
    put(q_ref, proj(2) * (HEAD_DIM ** -0.5))
    put(k_ref, proj(3))
    put(v_ref, proj(4))
    put(og_ref, _sigmoid(proj(5)))
    put(ga_ref, _sigmoid(proj(6)))
    put(gb_ref, _sigmoid(proj(7)))
    n_gates = gt_ref.shape[0]
    gates = jnp.dot(ub, wm_ref[:, 8 * d:], preferred_element_type=F32) + bm_ref[:, 8 * d:]
    gt_ref[...] = gates.T[0:n_gates, :]


def _ffn_in_kernel(x_ref, mod_ref, n1_ref, wg_ref, wu_ref, wd_ref, n2_ref, wm_ref, bm_ref,
                   x1_ref, *act_refs):
    mod = mod_ref[...]
    sh1, sc1, g1, sh2, sc2 = [mod[:, i:i + 1, :] for i in range(5)]
    x1 = _ffn_block(x_ref[...], n1_ref[...], sh1, sc1, g1, wg_ref, wu_ref, wd_ref)
    x1_ref[...] = x1
    _in_proj(x1, sh2, sc2, n2_ref, wm_ref, bm_ref, *act_refs)


def _ffn_in_call(x, mod_a, nrm1, wg, wu, wd, nrm2, w_main, b_main, *, nseq, t_tile, act_dtype, mod_block0):
    b, s, d = x.shape
    grid = (b // nseq, s // t_tile)
    rows = nseq * t_tile
    tok = pl.BlockSpec((nseq, t_tile, d), lambda i, j: (i, j, 0))
    n_t = s // t_tile
    in_specs = [
        tok,
        pl.BlockSpec((nseq, MOD_SPLIT, d), lambda i, j: (mod_block0 + i, 0, 0)),
        _const_spec((1, d)),
        _const_spec(wg.shape), _const_spec(wu.shape), _const_spec(wd.shape),
        _const_spec((1, d)),
        _const_spec(w_main.shape), _const_spec(b_main.shape),
    ]
    out_specs = [tok] * 8 + [pl.BlockSpec((2 * N_HEADS, rows), lambda i, j: (0, i * n_t + j))]
    out_shape = ([jax.ShapeDtypeStruct((b, s, d), F32)] * 2
                 + [jax.ShapeDtypeStruct((b, s, d), act_dtype)] * 6
                 + [jax.ShapeDtypeStruct((2 * N_HEADS, b * s), F32)])
    return pl.pallas_call(
        _ffn_in_kernel,
        grid=grid,
        in_specs=in_specs,
        out_specs=out_specs,
        out_shape=out_shape,
        compiler_params=pltpu.CompilerParams(
            dimension_semantics=("arbitrary", "arbitrary"), vmem_limit_bytes=VMEM_LIMIT_BIG),
        name="ffn_in",
    )(x, mod_a, nrm1, wg, wu, wd, nrm2, w_main, b_main)


def _seg_cumsum_lanes(x, seg):
    lane = lax.broadcasted_iota(jnp.int32, x.shape, 1)
    pos = lane & (LANES - 1)
    sh = 1
    while sh < seg:
        x = x + jnp.where(pos >= sh, pltpu.roll(x, sh, 1), 0.0)
        sh *= 2
    return x


def _mlstm_head_chunk(q, k, v, g_row, lf_row, c_st, n_row, m):
    l = q.shape[0]
    t_idx = lax.broadcasted_iota(jnp.int32, (l, l), 0)
    s_idx = lax.broadcasted_iota(jnp.int32, (l, l), 1)
    causal = s_idx <= t_idx
    gb = jnp.broadcast_to(g_row, (l, l))
    lfb = jnp.broadcast_to(lf_row, (l, l))
    cm_col = jnp.max(jnp.where(causal, gb, -jnp.inf), axis=1, keepdims=True)
    bcum_col = jnp.sum(jnp.where(causal, lfb, 0.0), axis=1, keepdims=True)
    g_col = jnp.sum(jnp.where(s_idx == t_idx, gb, 0.0), axis=1, keepdims=True)
    g_max = cm_col[l - 1:l, :]
    qb, kb, vb = q.astype(BF16), k.astype(BF16), v.astype(BF16)

    s = lax.dot_general(qb, kb, (((1,), (1,)), ((), ())), preferred_element_type=F32)
    p = (s * jnp.where(causal, jnp.exp(gb - cm_col), 0.0)).astype(BF16)
    pv = jnp.dot(p, vb, preferred_element_type=F32)
    den_loc = jnp.dot(p, jnp.ones((l, LANES), BF16), preferred_element_type=F32)
    ws_loc = jnp.exp(g_col - g_max)
    vs = (v.astype(F32) * ws_loc).astype(BF16)
    upd = lax.dot_general(vs, kb, (((0,), (0,)), ((), ())), preferred_element_type=F32)
    ws_rows = jnp.broadcast_to(jnp.exp(g_row - g_max), (SUBLANES, l)).astype(BF16)
    n_upd = jnp.dot(ws_rows, kb, preferred_element_type=F32)[0:1, :]

    m_col = jnp.maximum(m, cm_col)
    r_col = jnp.exp(cm_col - m_col)
    w_inter = jnp.exp(m - m_col)
    qc = lax.dot_general(qb, c_st.astype(BF16), (((1,), (1,)), ((), ())), preferred_element_type=F32)
    n_rep = jnp.broadcast_to(n_row, (LANES, n_row.shape[1])).astype(BF16)
    qn = lax.dot_general(qb, n_rep, (((1,), (1,)), ((), ())), preferred_element_type=F32)
    num = r_col * pv + w_inter * qc
    den = jnp.maximum(jnp.abs(r_col * den_loc + w_inter * qn), jnp.exp(-(bcum_col + m_col)))
    inv_den = 1.0 / den
    h = num * jnp.concatenate([inv_den] * (num.shape[1] // LANES), axis=1)
    m_last = m_col[l - 1:l, :]
    a_prev = jnp.exp(m - m_last)
    scale = jnp.exp(g_max - m_last)
    c_new = a_prev * c_st + scale * upd
    n_new = a_prev * n_row + scale * n_upd
    m_new = bcum_col[l - 1:l, :] + m_last
    return h, c_new, n_new, m_new


def _head_out(h, og, mh):
    hn = h * lax.rsqrt(jnp.mean(h * h, axis=-1, keepdims=True) + EPS) * mh
    if og.dtype == BF16:
        return og * hn.astype(BF16)
    return og * hn


CONV_ROW_BLOCK = 8 * SUBLANES


def _conv_prompt_tile(cbuf, xs, wdw_ref, bdw_ref, ycv_ref):
    n_rows, d = cbuf.shape
    t = n_rows - CONV_PAD
    span = n_rows - SUBLANES
    for lb in range(d // LANES):
        ls = slice(lb * LANES, (lb + 1) * LANES)
        xl = xs.at[lb % 2]
        for r in range(1, SUBLANES):
            xl[r, 0:span, :] = cbuf[r:r + span, ls]
        w_l = wdw_ref[:, ls]
        b_l = bdw_ref[:, ls]
        for r0 in range(0, t, CONV_ROW_BLOCK):
            acc = jnp.broadcast_to(b_l, (CONV_ROW_BLOCK, LANES))
            for j in range(CONV_WIDTH):
                r = (CONV_OFF + j) % SUBLANES
                lo = CONV_OFF + j - r + r0
                src = cbuf[lo:lo + CONV_ROW_BLOCK, ls] if r == 0 else xl[r, lo:lo + CONV_ROW_BLOCK, :]
                acc = acc + w_l[j:j + 1, :] * src
            ycv_ref[0, r0:r0 + CONV_ROW_BLOCK, ls] = acc


def _seqmix_prompt_kernel(q_ref, k_ref, v_ref, og_ref, gt_ref, glu_ref, mh_ref, wdw_ref, bdw_ref,
                          hb_ref, ycv_ref, c_out, n_out, m_out, cso_ref, c_s, n_s, m_s, cbuf, xs):
    t = pl.program_id(1)
    tm = q_ref.shape[1]

    @pl.when(t == 0)
    def _():
        c_s[...] = jnp.zeros_like(c_s)
        n_s[...] = jnp.zeros_like(n_s)
        m_s[...] = jnp.zeros_like(m_s)
        cbuf[0:CONV_PAD, :] = jnp.zeros((CONV_PAD, cbuf.shape[1]), F32)

    @pl.when(t > 0)
    def _():
        cbuf[0:CONV_PAD, :] = cbuf[tm:tm + CONV_PAD, :]

    cbuf[CONV_PAD:CONV_PAD + tm, :] = glu_ref[0]
    _conv_prompt_tile(cbuf, xs, wdw_ref, bdw_ref, ycv_ref)
    cso_ref[0] = cbuf[CONV_OFF + tm:CONV_PAD + tm, :]

    g8 = gt_ref[...]
    lf8 = _log_sigmoid(g8)
    bc8 = _seg_cumsum_lanes(lf8, CHUNK)
    lf = lf8[N_HEADS:2 * N_HEADS, :]
    gg = g8[0:N_HEADS, :] - bc8[N_HEADS:2 * N_HEADS, :]
    for c in range(tm // CHUNK):
        rows = slice(c * CHUNK, (c + 1) * CHUNK)
        for hd in range(N_HEADS):
            cols = slice(hd * HEAD_DIM, (hd + 1) * HEAD_DIM)
            h, c_new, n_new, m_new = _mlstm_head_chunk(
                q_ref[0, rows, cols], k_ref[0, rows, cols], v_ref[0, rows, cols],
                gg[hd:hd + 1, rows], lf[hd:hd + 1, rows],
                c_s[hd], n_s[hd:hd + 1, :], m_s[hd:hd + 1, 0:1])
            c_s[hd] = c_new
            n_s[hd:hd + 1, :] = n_new
            m_s[hd:hd + 1, :] = jnp.broadcast_to(m_new, (1, LANES))
            hb_ref[0, rows, cols] = _head_out(h, og_ref[0, rows, cols], mh_ref[:, cols]).astype(hb_ref.dtype)

    @pl.when(t == pl.num_programs(1) - 1)
    def _():
        c_out[0] = c_s[...]
        n_out[0] = n_s[0:N_HEADS, :]
        m_out[0] = m_s[...]


def _seqmix_prompt_call(q, k, v, og, g_t, glu, mh, w_dw, b_dw, *, t_tile):
    b, s, d = q.shape
    n_t = s // t_tile
    tok = pl.BlockSpec((1, t_tile, d), lambda i, j: (i, j, 0))
    return pl.pallas_call(
        _seqmix_prompt_kernel,
        grid=(b, n_t),
        in_specs=[tok, tok, tok, tok,
                  pl.BlockSpec((2 * N_HEADS, t_tile), lambda i, j: (0, i * n_t + j)),
                  tok, _const_spec((1, d)), _const_spec(w_dw.shape), _const_spec((1, d))],
        out_specs=[tok, tok,
                   pl.BlockSpec((1, N_HEADS, HEAD_DIM, HEAD_DIM), lambda i, j: (i, 0, 0, 0)),
                   pl.BlockSpec((1, N_HEADS, HEAD_DIM), lambda i, j: (i, 0, 0)),
                   pl.BlockSpec((1, SUBLANES, LANES), lambda i, j: (i, 0, 0)),
                   pl.BlockSpec((None, 1, CONV_STATE, d), lambda i, j: (0, i, 0, 0))],
        out_shape=[jax.ShapeDtypeStruct((b, s, d), BF16),
                   jax.ShapeDtypeStruct((b, s, d), F32),
                   jax.ShapeDtypeStruct((b, N_HEADS, HEAD_DIM, HEAD_DIM), F32),
                   jax.ShapeDtypeStruct((b, N_HEADS, HEAD_DIM), F32),
                   jax.ShapeDtypeStruct((b, SUBLANES, LANES), F32),
                   jax.ShapeDtypeStruct((1, b, CONV_STATE, d), F32)],
        scratch_shapes=[pltpu.VMEM((N_HEADS, HEAD_DIM, HEAD_DIM), F32),
                        pltpu.VMEM((SUBLANES, HEAD_DIM), F32),
                        pltpu.VMEM((SUBLANES, LANES), F32),
                        pltpu.VMEM((CONV_PAD + t_tile, d), F32),
                        pltpu.VMEM((2, SUBLANES, CONV_PAD + t_tile, LANES), F32)],
        compiler_params=pltpu.CompilerParams(
            dimension_semantics=("arbitrary", "arbitrary"), vmem_limit_bytes=VMEM_LIMIT_SMALL),
        name="seqmix_prompt",
    )(q, k, v, og, g_t, glu, mh, w_dw, b_dw)


def _seqmix_sample_kernel(q_ref, k_ref, v_ref, og_ref, gr_ref, c_in, n_in, m_in, glu_ref, cst_ref,
                          mh_ref, wdw_ref, bdw_ref,
                          hb_ref, ycv_ref, c_out, n_out, m_out, cso_ref, cbuf):
    sb, t, d = q_ref.shape
    cbuf[:, CONV_OFF:CONV_PAD, :] = cst_ref[...]
    cbuf[:, CONV_PAD:CONV_PAD + t, :] = glu_ref[...]
    assert t == SUBLANES
    row_id = lax.broadcasted_iota(jnp.int32, (SUBLANES, LANES), 0)
    for lb in range(d // LANES):
        ls = slice(lb * LANES, (lb + 1) * LANES)
        w_l = wdw_ref[:, ls]
        b_l = jnp.broadcast_to(bdw_ref[:, ls], (SUBLANES, LANES))
        for s in range(sb):
            groups = [cbuf[s, g * SUBLANES:(g + 1) * SUBLANES, ls] for g in range((CONV_PAD + t) // SUBLANES)]
            acc = b_l
            for j in range(CONV_WIDTH):
                g, r = divmod(CONV_OFF + j, SUBLANES)
                if r == 0:
                    win = groups[g]
                else:
                    win = pltpu.roll(jnp.where(row_id >= r, groups[g], groups[g + 1]), SUBLANES - r, 0)
                acc = acc + w_l[j:j + 1, :] * win
            ycv_ref[s, :, ls] = acc
    cso_ref[...] = cbuf[:, CONV_OFF + t:CONV_PAD + t, :]

    lane = lax.broadcasted_iota(jnp.int32, (SUBLANES, LANES), 1)

    for s in range(sb):
        g8 = gr_ref[s]
        lf8 = _log_sigmoid(g8)
        bc8 = _seg_cumsum_lanes(lf8, t)
        lf = lf8[N_HEADS:2 * N_HEADS, :]
        gg = g8[0:N_HEADS, :] - bc8[N_HEADS:2 * N_HEADS, :]
        m_blk = m_in[s]
        m_acc = jnp.zeros((SUBLANES, LANES), F32)
        for hd in range(N_HEADS):
            cols = slice(hd * HEAD_DIM, (hd + 1) * HEAD_DIM)
            h, c_new, n_new, m_new = _mlstm_head_chunk(
                q_ref[s, :, cols], k_ref[s, :, cols], v_ref[s, :, cols],
                gg[hd:hd + 1, 0:t], lf[hd:hd + 1, 0:t],
                c_in[s, hd], n_in[s, hd:hd + 1, :], m_blk[0:1, hd:hd + 1])
            c_out[s, hd] = c_new
            n_out[s, hd:hd + 1, :] = n_new
            m_acc = jnp.where(lane == hd, m_new, m_acc)
            hb_ref[s, :, cols] = _head_out(h, og_ref[s, :, cols], mh_ref[:, cols]).astype(hb_ref.dtype)
        m_out[s] = m_acc


def _seqmix_sample_call(q, k, v, og, g_r, c0, n0, m0, glu, conv_state, mh, w_dw, b_dw, *, seq_block):
    b, t, d = q.shape
    tok = pl.BlockSpec((seq_block, t, d), lambda i: (i, 0, 0))
    small = pl.BlockSpec((seq_block, SUBLANES, LANES), lambda i: (i, 0, 0))
    c_spec = pl.BlockSpec((seq_block, N_HEADS, HEAD_DIM, HEAD_DIM), lambda i: (i, 0, 0, 0))
    n_spec = pl.BlockSpec((seq_block, N_HEADS, HEAD_DIM), lambda i: (i, 0, 0))
    st_spec = pl.BlockSpec((None, seq_block, CONV_STATE, d), lambda i: (0, i, 0, 0))
    return pl.pallas_call(
        _seqmix_sample_kernel,
        grid=(b // seq_block,),
        in_specs=[tok, tok, tok, tok, small, c_spec, n_spec, small, tok, st_spec,
                  _const_spec((1, d)), _const_spec(w_dw.shape), _const_spec((1, d))],
        out_specs=[tok, tok, c_spec, n_spec, small, st_spec],
        out_shape=[jax.ShapeDtypeStruct((b, t, d), F32),
                   jax.ShapeDtypeStruct((b, t, d), F32),
                   jax.ShapeDtypeStruct(c0.shape, F32),
                   jax.ShapeDtypeStruct(n0.shape, F32),
                   jax.ShapeDtypeStruct((b, SUBLANES, LANES), F32),
                   jax.ShapeDtypeStruct((1, b, CONV_STATE, d), F32)],
        scratch_shapes=[pltpu.VMEM((seq_block, CONV_PAD + t, d), F32)],
        compiler_params=pltpu.CompilerParams(
            dimension_semantics=("arbitrary",), vmem_limit_bytes=VMEM_LIMIT_BIG),
        name="seqmix_sample",
    )(q, k, v, og, g_r, c0, n0, m0, glu, conv_state, mh, w_dw, b_dw)


def _mix_tail(x1, yc3, hb, ga, gb, mod, lng_ref, lnb_ref, wco_ref, wmo_ref, wo_ref,
              n3_ref, wg_ref, wu_ref, wd_ref, fn_ref):
    ns, t, d = x1.shape
    rows = ns * t
    yc = yc3.reshape(rows, d)
    mu = jnp.mean(yc, axis=-1, keepdims=True)
    yd = yc - mu
    var = jnp.mean(yd * yd, axis=-1, keepdims=True)
    ln = yd * lax.rsqrt(var + EPS) * lng_ref[...] + lnb_ref[...]
    a_in = (ln * _sigmoid(ln)).astype(BF16)
    a_out = jnp.dot(a_in, wco_ref[...], preferred_element_type=F32)
    b_out = jnp.dot(hb.reshape(rows, d).astype(BF16), wmo_ref[...], preferred_element_type=F32)
    z = (ga.reshape(rows, d).astype(F32) * a_out + gb.reshape(rows, d).astype(F32) * b_out).astype(BF16)
    zo = jnp.dot(z, wo_ref[...], preferred_element_type=F32)
    g2, sh3, sc3, g3 = [mod[:, i:i + 1, :] for i in range(4)]
    x2 = x1 + g2 * zo.reshape(ns, t, d)
    x3 = _ffn_block(x2, n3_ref[...], sh3, sc3, g3, wg_ref, wu_ref, wd_ref)
    return _rms(x3, fn_ref[...])


def _tail_kernel(x1_ref, ycv_ref, hb_ref, ga_ref, gb_ref, mod_ref, *rest):
    *tail_w, y_ref = rest
    y_ref[...] = _mix_tail(x1_ref[...], ycv_ref[...], hb_ref[...], ga_ref[...], gb_ref[...], mod_ref[...], *tail_w)


def _tail_call(x1, ycv, hb, ga, gb, mod_b, weights, *, nseq, t_tile, mod_block0):
    b, s, d = x1.shape
    tok = pl.BlockSpec((nseq, t_tile, d), lambda i, j: (i, j, 0))
    in_specs = ([tok] * 5 + [pl.BlockSpec((nseq, N_MOD - MOD_SPLIT, d), lambda i, j: (mod_block0 + i, 0, 0))]
                + [_const_spec(w.shape) for w in weights])
    return pl.pallas_call(
        _tail_kernel,
        grid=(b // nseq, s // t_tile),
        in_specs=in_specs,
        out_specs=tok,
        out_shape=jax.ShapeDtypeStruct((b, s, d), F32),
        compiler_params=pltpu.CompilerParams(
            dimension_semantics=("arbitrary", "arbitrary"), vmem_limit_bytes=VMEM_LIMIT_BIG),
        name="tail",
    )(x1, ycv, hb, ga, gb, mod_b, *weights)


def _layer(xp, xs, c_all, st_conv, st_c, st_n, st_m, final_norm,
           w_ada, b_ada, norm_ffn1, w1_gate, w1_up, w1_down, norm_mix, w_in, b_in, w_dw, b_dw,
           ln_conv_g, ln_conv_b, w_conv_out, mh_norm, w_mlstm_out, w_out, norm_ffn2, w2_gate, w2_up, w2_down):
    bp = xp.shape[0]
    bs, ts, d = xs.shape
    row = lambda a: a.reshape(1, -1)

    mod_a, mod_b = _mod_call(c_all, w_ada, b_ada)

    n_main = 8 * d
    assert w_in.shape[1] == n_main + 2 * N_HEADS
    w_main = _in_weight_call(w_in.T, IN_WEIGHT_BLOCKS)
    b_main = row(jnp.pad(b_in, (0, w_main.shape[1] - w_in.shape[1])))
    ffn1 = (row(norm_ffn1), w1_gate.astype(BF16), w1_up.astype(BF16), w1_down.astype(BF16))
    in_w = (row(norm_mix), w_main, b_main)
    conv_w = (row(mh_norm), jnp.concatenate([w_dw, jnp.zeros((CONV_PAD - CONV_WIDTH, d), F32)], axis=0), row(b_dw))
    tail_w = (row(ln_conv_g), row(ln_conv_b),
              w_conv_out.astype(BF16), w_mlstm_out.astype(BF16), w_out.astype(BF16),
              row(norm_ffn2), w2_gate.astype(BF16), w2_up.astype(BF16), w2_down.astype(BF16), row(final_norm))

    x1, glu, q, k, v, og, ga, gb, g_t = _ffn_in_call(
        xp, mod_a, *ffn1, *in_w, nseq=1, t_tile=256, act_dtype=BF16, mod_block0=bs)
    hb, ycv, c_p, n_p, m_p, conv_p = _seqmix_prompt_call(q, k, v, og, g_t, glu, *conv_w, t_tile=512)
    y_p = _tail_call(x1, ycv, hb, ga, gb, mod_b, tail_w, nseq=1, t_tile=512, mod_block0=bs)

    x1, glu, q, k, v, og, ga, gb, g_t = _ffn_in_call(
        xs, mod_a, *ffn1, *in_w, nseq=32, t_tile=ts, act_dtype=F32, mod_block0=0)
    g_r = g_t.reshape(2 * N_HEADS, bs, ts).transpose(1, 0, 2)
    g_r = jnp.pad(g_r, ((0, 0), (0, 0), (0, LANES - ts)))
    m_b = jnp.broadcast_to(jnp.pad(st_m, ((0, 0), (0, LANES - N_HEADS)))[:, None, :], (bs, SUBLANES, LANES))
    hb, ycv, c_s, n_s, m_s, conv_s = _seqmix_sample_call(
        q, k, v, og, g_r, st_c, st_n, m_b, glu, st_conv, *conv_w, seq_block=8)
    y_s = _tail_call(x1, ycv, hb, ga, gb, mod_b, tail_w, nseq=32, t_tile=ts, mod_block0=0)

    return (y_p, y_s, conv_p, c_p, n_p, m_p[:, :N_HEADS, 0], conv_s, c_s, n_s, m_s[:, 0, :N_HEADS])


def kernel(x_prompt, x_sample, c_prompt, c_sample, state_conv, state_C, state_n, state_m, w_ada, b_ada, norm_ffn1, w1_gate, w1_up, w1_down, norm_mix, w_in, b_in, w_dw, b_dw, ln_conv_g, ln_conv_b, w_conv_out, mh_norm, w_mlstm_out, w_out, norm_ffn2, w2_gate, w2_up, w2_down, final_norm):
    depth = w_ada.shape[0]
    assert depth == 1, "the fused final norm assumes a single layer"
    c_all = jnp.concatenate([c_sample, c_prompt], axis=0)
    layer_w = (w_ada, b_ada, norm_ffn1, w1_gate, w1_up, w1_down, norm_mix, w_in, b_in, w_dw, b_dw,
               ln_conv_g, ln_conv_b, w_conv_out, mh_norm, w_mlstm_out, w_out, norm_ffn2, w2_gate, w2_up, w2_down)
    outs = _layer(x_prompt, x_sample, c_all, state_conv, state_C[0], state_n[0], state_m[0], final_norm,
                  *[w[0] for w in layer_w])
    conv_idx = (2, 6)
    return tuple(o if i < 2 or i in conv_idx else o[None] for i, o in enumerate(outs))
```
